```python
import math
import jax, jax.numpy as jnp
from jax import lax
import numpy as np

D_MODEL = 1024
BATCH = 16
SEQ = 2048
DEPTH = 4

CHUNK = 64
Q_BLOCK = 128
N_HEADS = 8
QK_NOPE = 64
QK_ROPE = 32
V_HEAD = 64
Q_LORA = 384
KV_LORA = 256
ROPE_THETA = 10000.0
SSM_WIDTH = 512
SSM_GROUP = 16
SSM_GROUPS = SSM_WIDTH // SSM_GROUP
SSM_STATE = 64
DT_MIN = 1e-3
DT_MAX = 1e-1
D_FF = 2816
CONV_W = 3
EPS = 1e-6
N_BRANCH = 2
IN_COLS = Q_LORA + KV_LORA + QK_ROPE + SSM_WIDTH + N_BRANCH * D_MODEL

kernel_name = "hybrid_mla_s5_convglu_trunk"


def rmsnorm(x, g):
    xf = x.astype(jnp.float32)
    var = jnp.mean(xf * xf, axis=-1, keepdims=True)
    return (xf * lax.rsqrt(var + EPS) * g.astype(jnp.float32)).astype(x.dtype)


def rope_tables(seq):
    pos = jnp.arange(seq, dtype=jnp.float32)
    inv_freq = ROPE_THETA ** (-jnp.arange(0, QK_ROPE, 2, dtype=jnp.float32) / QK_ROPE)
    ang = pos[:, None] * inv_freq[None, :]
    return jnp.cos(ang), jnp.sin(ang)


def apply_rope(x, cos, sin):
    xf = x.astype(jnp.float32)
    x1, x2 = jnp.split(xf, 2, axis=-1)
    out = jnp.concatenate([x1 * cos - x2 * sin, x2 * cos + x1 * sin], axis=-1)
    return out.astype(x.dtype)


def mla_branch(c_q, c_kv, k_pe, g_q, w_uq, g_kv, w_ukv, w_o_att, cos, sin):
    b, s, _ = c_q.shape
    q = (rmsnorm(c_q, g_q) @ w_uq).reshape(b, s, N_HEADS, QK_NOPE + QK_ROPE)
    q_nope, q_pe = q[..., :QK_NOPE], q[..., QK_NOPE:]
    q_pe = apply_rope(q_pe, cos[None, :, None, :], sin[None, :, None, :])
    kv = (rmsnorm(c_kv, g_kv) @ w_ukv).reshape(b, s, N_HEADS, QK_NOPE + V_HEAD)
    k_nope, v = kv[..., :QK_NOPE], kv[..., QK_NOPE:]
    k_pe = apply_rope(k_pe, cos[None], sin[None])
    scale = 1.0 / math.sqrt(QK_NOPE + QK_ROPE)
    outs = []
    for i in range(s // Q_BLOCK):
        q0 = i * Q_BLOCK
        kend = q0 + Q_BLOCK
        sc = (jnp.einsum('bqhd,bkhd->bhqk', q_nope[:, q0:kend], k_nope[:, :kend])
              + jnp.einsum('bqhr,bkr->bhqk', q_pe[:, q0:kend], k_pe[:, :kend]))
        sc = sc.astype(jnp.float32) * scale
        q_chunk = (q0 + jnp.arange(Q_BLOCK)) // CHUNK
        k_chunk = jnp.arange(kend) // CHUNK
        mask = q_chunk[:, None] >= k_chunk[None, :]
        sc = jnp.where(mask[None, None], sc, -jnp.inf)
        p = jax.nn.softmax(sc, axis=-1).astype(v.dtype)
        outs.append(jnp.einsum('bhqk,bkhd->bqhd', p, v[:, :kend]))
    o = jnp.concatenate(outs, axis=1).reshape(b, s, N_HEADS * V_HEAD)
    return o @ w_o_att


def s5_combine(e1, e2):
    a1r, a1i, b1r, b1i = e1
    a2r, a2i, b2r, b2i = e2
    return (a2r * a1r - a2i * a1i,
            a2r * a1i + a2i * a1r,
            a2r * b1r - a2i * b1i + b2r,
            a2r * b1i + a2i * b1r + b2i)


def s5_branch(u, lam_re, lam_im, log_step, b_re, b_im, c_re, c_im, d_skip, w_glu):
    b, s, _ = u.shape
    u32 = u.astype(jnp.float32)
    ug = u32.reshape(b, s, SSM_GROUPS, SSM_GROUP)
    lr = lam_re.astype(jnp.float32)
    li = lam_im.astype(jnp.float32)
    delta = jnp.exp(log_step.astype(jnp.float32))[:, None]
    mag = jnp.exp(lr * delta)
    abar_re = mag * jnp.cos(li * delta)
    abar_im = mag * jnp.sin(li * delta)
    nr, ni = abar_re - 1.0, abar_im
    den = lr * lr + li * li
    f_re = (nr * lr + ni * li) / den
    f_im = (ni * lr - nr * li) / den
    br, bi = b_re.astype(jnp.float32), b_im.astype(jnp.float32)
    bbar_re = f_re[..., None] * br - f_im[..., None] * bi
    bbar_im = f_re[..., None] * bi + f_im[..., None] * br
    bu_re = jnp.einsum('bsgh,gph->bsgp', ug, bbar_re)
    bu_im = jnp.einsum('bsgh,gph->bsgp', ug, bbar_im)
    a_re = jnp.broadcast_to(abar_re[None, None], (1, s, SSM_GROUPS, SSM_STATE))
    a_im = jnp.broadcast_to(abar_im[None, None], (1, s, SSM_GROUPS, SSM_STATE))
    _, _, xr, xi = lax.associative_scan(s5_combine, (a_re, a_im, bu_re, bu_im), axis=1)
    y = (jnp.einsum('bsgp,ghp->bsgh', xr, c_re.astype(jnp.float32))
         - jnp.einsum('bsgp,ghp->bsgh', xi, c_im.astype(jnp.float32)))
    y = y.reshape(b, s, SSM_WIDTH) + d_skip.astype(jnp.float32) * u32
    y = jax.nn.gelu(y).astype(u.dtype)
    ga, gb = jnp.split(y @ w_glu, 2, axis=-1)
    return ga * jax.nn.sigmoid(gb)


def conv_glu_ffn(h, w_up, conv_w, conv_b, w_down):
    s = h.shape[1]
    gate, val = jnp.split(h @ w_up, 2, axis=-1)
    padded = jnp.pad(gate, ((0, 0), (CONV_W - 1, 0), (0, 0)))
    conv = conv_b
    for k in range(CONV_W):
        conv = conv + conv_w[k] * padded[:, k:k + s]
    return (jax.nn.gelu(conv) * val) @ w_down


def setup_inputs(seed: int = 0) -> dict:
    key = jax.random.key(seed)
    ks = jax.random.split(key, 24)
    f32 = jnp.float32
    nrm = lambda k, shape, sc: jax.random.normal(k, shape, f32) * sc
    gain = lambda k, dim: 1.0 + 0.05 * jax.random.normal(k, (DEPTH, dim), f32)
    n_idx = jnp.arange(SSM_STATE, dtype=f32)
    return {
        "x": jax.random.normal(ks[0], (BATCH, SEQ, D_MODEL), f32),
        "w_in": nrm(ks[1], (DEPTH, D_MODEL, IN_COLS), D_MODEL ** -0.5),
        "b_gate": nrm(ks[2], (DEPTH, N_BRANCH * D_MODEL), 0.1),
        "g_mix_pre": gain(ks[3], D_MODEL),
        "g_q": gain(ks[4], Q_LORA),
        "w_uq": nrm(ks[5], (DEPTH, Q_LORA, N_HEADS * (QK_NOPE + QK_ROPE)), Q_LORA ** -0.5),
        "g_kv": gain(ks[6], KV_LORA),
        "w_ukv": nrm(ks[7], (DEPTH, KV_LORA, N_HEADS * (QK_NOPE + V_HEAD)), KV_LORA ** -0.5),
        "w_o_att": nrm(ks[8], (DEPTH, N_HEADS * V_HEAD, D_MODEL), (N_HEADS * V_HEAD) ** -0.5),
        "lam_re": -0.5 + 0.01 * jax.random.normal(ks[9], (DEPTH, SSM_GROUPS, SSM_STATE), f32),
        "lam_im": math.pi * n_idx + 0.01 * jax.random.normal(ks[10], (DEPTH, SSM_GROUPS, SSM_STATE), f32),
        "log_step": jax.random.uniform(ks[11], (DEPTH, SSM_GROUPS), f32,
                                       math.log(DT_MIN), math.log(DT_MAX)),
        "b_re": nrm(ks[12], (DEPTH, SSM_GROUPS, SSM_STATE, SSM_GROUP), (2 * SSM_GROUP) ** -0.5),
        "b_im": nrm(ks[13], (DEPTH, SSM_GROUPS, SSM_STATE, SSM_GROUP), (2 * SSM_GROUP) ** -0.5),
        "c_re": nrm(ks[14], (DEPTH, SSM_GROUPS, SSM_GROUP, SSM_STATE), (2 * SSM_STATE) ** -0.5),
        "c_im": nrm(ks[15], (DEPTH, SSM_GROUPS, SSM_GROUP, SSM_STATE), (2 * SSM_STATE) ** -0.5),
        "d_skip": nrm(ks[16], (DEPTH, SSM_WIDTH), 1.0),
        "w_glu": nrm(ks[17], (DEPTH, SSM_WIDTH, 2 * D_MODEL), SSM_WIDTH ** -0.5),
        "w_out": nrm(ks[18], (DEPTH, D_MODEL, D_MODEL), D_MODEL ** -0.5),
        "g_mix_post": gain(ks[19], D_MODEL),
        "g_ffn_pre": gain(ks[20], D_MODEL),
        "w_up": nrm(ks[21], (DEPTH, D_MODEL, 2 * D_FF), D_MODEL ** -0.5),
        "conv_w": nrm(ks[22], (DEPTH, CONV_W, D_FF), CONV_W ** -0.5),
        "conv_b": nrm(ks[23], (DEPTH, D_FF), 0.02),
        "w_down": nrm(jax.random.fold_in(key, 101), (DEPTH, D_FF, D_MODEL), D_FF ** -0.5),
        "g_ffn_post": gain(jax.random.fold_in(key, 102), D_MODEL),
    }


def reference(x, w_in, b_gate, g_mix_pre, g_q, w_uq, g_kv, w_ukv, w_o_att,
              lam_re, lam_im, log_step, b_re, b_im, c_re, c_im, d_skip, w_glu,
              w_out, g_mix_post, g_ffn_pre, w_up, conv_w, conv_b, w_down, g_ffn_post):
    s = x.shape[1]
    cos, sin = rope_tables(s)
    o1 = Q_LORA
    o2 = o1 + KV_LORA
    o3 = o2 + QK_ROPE
    o4 = o3 + SSM_WIDTH
    for l in range(DEPTH):
        h = rmsnorm(x, g_mix_pre[l])
        z = h @ w_in[l]
        c_q, c_kv, k_pe, u = z[..., :o1], z[..., o1:o2], z[..., o2:o3], z[..., o3:o4]
        gates = jax.nn.sigmoid((z[..., o4:] + b_gate[l]).astype(jnp.float32)).astype(x.dtype)
        g_att, g_ssm = gates[..., :D_MODEL], gates[..., D_MODEL:]
        y_att = mla_branch(c_q, c_kv, k_pe, g_q[l], w_uq[l], g_kv[l], w_ukv[l],
                           w_o_att[l], cos, sin)
        y_ssm = s5_branch(u, lam_re[l], lam_im[l], log_step[l], b_re[l], b_im[l],
                          c_re[l], c_im[l], d_skip[l], w_glu[l])
        merged = g_att * y_att + g_ssm * y_ssm
        x = x + rmsnorm(merged @ w_out[l], g_mix_post[l])
        h = rmsnorm(x, g_ffn_pre[l])
        x = x + rmsnorm(conv_glu_ffn(h, w_up[l], conv_w[l], conv_b[l], w_down[l]), g_ffn_post[l])
    return x
```

```python
import functools
import math

import jax
import jax.numpy as jnp
from jax import lax
from jax.experimental import pallas as pl
from jax.experimental.pallas import tpu as pltpu

D_MODEL = 1024
BATCH = 16
SEQ = 2048
DEPTH = 4
CHUNK = 64
N_HEADS = 8
QK_NOPE = 64
QK_ROPE = 32
V_HEAD = 64
Q_LORA = 384
KV_LORA = 256
ROPE_THETA = 10000.0
SSM_WIDTH = 512
SSM_GROUP = 16
SSM_GROUPS = SSM_WIDTH // SSM_GROUP
SSM_STATE = 64
D_FF = 2816
CONV_W = 3
EPS = 1e-6

LANES = 128
HEAD_PAD = LANES
S5_L = 16
S5_CHUNKS = SEQ // S5_L
S5_PAIRS = SSM_GROUPS // 2
S5_BLK = S5_L * SSM_GROUP
VMEM_LIMIT = 56 * 1024 * 1024

F32 = jnp.float32
BF16 = jnp.bfloat16


def _rms(x, g):
    var = jnp.mean(x * x, axis=-1, keepdims=True)
    return x * lax.rsqrt(var + EPS) * g


def _dot(a, b):
    return jnp.dot(a, b, preferred_element_type=F32)


def _dot_nt(a, b):
    return lax.dot_general(a, b, (((1,), (1,)), ((), ())), preferred_element_type=F32)


C_Q0, C_KV0, C_U0, C_GA0, C_GS0, C_KA0, C_KB0, C_END = 0, 384, 640, 1152, 2176, 3200, 3328, 3456


def _mix_in_kernel(x_ref, gpre_ref, win_ref, bga_ref, bgs_ref, gq_ref, wuq_ref, wuqs_ref,
                   gkv_ref, wk_ref, wksel_ref, wv_ref, cq_ref, sq_ref, ck_ref, sk_ref,
                   q_ref, k_ref, v_ref, u_ref, ga_ref, gs_ref):
    h = _rms(x_ref[...], gpre_ref[...]).astype(BF16)

    c_q = _dot(h, win_ref[:, C_Q0:C_KV0])
    cqn = _rms(c_q, gq_ref[...]).astype(BF16)
    q = _dot(cqn, wuq_ref[...])
    qs = _dot(cqn, wuqs_ref[...])
    cq, sq = cq_ref[...], sq_ref[...]
    for hd in range(N_HEADS):
        sl = slice(hd * HEAD_PAD, (hd + 1) * HEAD_PAD)
        q_ref[:, sl] = (q[:, sl] * cq + qs[:, sl] * sq).astype(BF16)

    c_kv = _dot(h, win_ref[:, C_KV0:C_U0])
    ckvn = _rms(c_kv, gkv_ref[...]).astype(BF16)
    kpe = (_dot(h, win_ref[:, C_KA0:C_KB0]) * ck_ref[...]
           + _dot(h, win_ref[:, C_KB0:C_END]) * sk_ref[...]).astype(BF16)
    k_ref[...] = (_dot(ckvn, wk_ref[...]) + _dot(kpe, wksel_ref[...])).astype(BF16)
    v_ref[...] = _dot(ckvn, wv_ref[...]).astype(BF16)

    u_ref[...] = _dot(h, win_ref[:, C_U0:C_GA0]).astype(BF16)
    ga_ref[...] = jax.nn.sigmoid(_dot(h, win_ref[:, C_GA0:C_GS0]) + bga_ref[...]).astype(BF16)
    gs_ref[...] = jax.nn.sigmoid(_dot(h, win_ref[:, C_GS0:C_KA0]) + bgs_ref[...]).astype(BF16)


def _mix_in(x, gpre, win, bga, bgs, gq, wuq, wuqs, gkv, wk, wksel, wv, cq, sq, ck, sk, tm=512):
    nt = SEQ // tm
    row = lambda d: pl.BlockSpec((None, tm, d), lambda i, b: (b, i, 0))
    full = lambda a: pl.BlockSpec(a.shape, lambda i, b: (0,) * a.ndim)
    tab = pl.BlockSpec((tm, LANES), lambda i, b: (i, 0))
    hq = N_HEADS * HEAD_PAD
    outs = [(hq, BF16), (hq, BF16), (N_HEADS * V_HEAD, BF16), (SSM_WIDTH, BF16),
            (D_MODEL, BF16), (D_MODEL, BF16)]
    return pl.pallas_call(
        _mix_in_kernel,
        grid=(nt, BATCH),
        in_specs=[row(D_MODEL), full(gpre), full(win), full(bga), full(bgs), full(gq), full(wuq),
                  full(wuqs), full(gkv), full(wk), full(wksel), full(wv), tab, tab, tab, tab],
        out_specs=[row(d) for d, _ in outs],
        out_shape=[jax.ShapeDtypeStruct((BATCH, SEQ, d), t) for d, t in outs],
        compiler_params=pltpu.CompilerParams(
            dimension_semantics=("arbitrary", "arbitrary"), vmem_limit_bytes=VMEM_LIMIT),
        name="mix_in",
    )(x, gpre, win, bga, bgs, gq, wuq, wuqs, gkv, wk, wksel, wv, cq, sq, ck, sk)


ATT_TQ = 256


def _attn_kernel(q_ref, k_ref, v_ref, o_ref):
    tq = ATT_TQ
    row_chunk = lax.broadcasted_iota(jnp.int32, (tq, tq), 0) // CHUNK
    col_chunk = lax.broadcasted_iota(jnp.int32, (tq, tq), 1) // CHUNK
    mask = row_chunk >= col_chunk
    lane = lax.broadcasted_iota(jnp.int32, (tq, 2 * V_HEAD), 1)
    for i in range(SEQ // tq):
        q0 = i * tq
        outs = []
        for hh in range(2):
            hs = slice(hh * HEAD_PAD, (hh + 1) * HEAD_PAD)
            qh = q_ref[q0:q0 + tq, hs]
            s_d = jnp.where(mask, _dot_nt(qh, k_ref[q0:q0 + tq, hs]), -jnp.inf)
            m = jnp.max(s_d, axis=-1, keepdims=True)
            if i > 0:
                s_p = _dot_nt(qh, k_ref[0:q0, hs])
                m = jnp.maximum(m, jnp.max(s_p, axis=-1, keepdims=True))
            p_d = jnp.exp(s_d - m)
            l = jnp.sum(p_d, axis=-1, keepdims=True)
            acc = _dot(p_d.astype(BF16), v_ref[q0:q0 + tq, :])
            if i > 0:
                p_p = jnp.exp(s_p - m)
                l = l + jnp.sum(p_p, axis=-1, keepdims=True)
                acc = acc + _dot(p_p.astype(BF16), v_ref[0:q0, :])
            outs.append(acc / l)
        o_ref[q0:q0 + tq, :] = jnp.where(lane < V_HEAD, outs[0], outs[1]).astype(BF16)


def _attn(q, k, v):
    blk = lambda d: pl.BlockSpec((None, SEQ, d), lambda b, p: (b, 0, p))
    return pl.pallas_call(
        _attn_kernel,
        grid=(BATCH, N_HEADS // 2),
        in_specs=[blk(2 * HEAD_PAD), blk(2 * HEAD_PAD), blk(2 * V_HEAD)],
        out_specs=blk(2 * V_HEAD),
        out_shape=jax.ShapeDtypeStruct((BATCH, SEQ, N_HEADS * V_HEAD), BF16),
        compiler_params=pltpu.CompilerParams(
            dimension_semantics=("arbitrary", "arbitrary"), vmem_limit_bytes=VMEM_LIMIT),
        name="attn",
    )(q, k, v)


S5_PREP_GB = 8


def _s5_prep_kernel(lr_r, li_r, ls_r, lr_c, li_c, ls_c, btr_ref, bti_ref, ctr_ref, cti_ref,
                    winr_ref, wini_ref, wintra_ref, woutr_ref, wouti_ref, a16r_ref, a16i_ref):
    srow = (lax.broadcasted_iota(jnp.int32, (S5_BLK, 1), 0) // SSM_GROUP).astype(F32)
    tcol = (lax.broadcasted_iota(jnp.int32, (1, S5_BLK), 1) // SSM_GROUP).astype(F32)
    tblk = lax.broadcasted_iota(jnp.int32, (S5_BLK, S5_BLK), 1) // SSM_GROUP
    for g in range(S5_PREP_GB):
        lr, li = lr_r[g], li_r[g]
        dt = jnp.exp(ls_r[g])
        mag = jnp.exp(lr * dt)
        abar_re = mag * jnp.cos(li * dt)
        abar_im = mag * jnp.sin(li * dt)
        nr, ni = abar_re - 1.0, abar_im
        den = lr * lr + li * li
        f_re = (nr * lr + ni * li) / den
        f_im = (ni * lr - nr * li) / den
        btr, bti = btr_ref[g], bti_ref[g]
        bbar_re = f_re * btr - f_im * bti
        bbar_im = f_re * bti + f_im * btr
        e_in = (S5_L - 1.0) - srow
        pm = jnp.exp(e_in * (lr * dt))
        pr = pm * jnp.cos(e_in * (li * dt))
        pi = pm * jnp.sin(e_in * (li * dt))
        win_re = bbar_re * pr - bbar_im * pi
        win_im = bbar_re * pi + bbar_im * pr
        winr_ref[g] = win_re.astype(BF16)
        wini_ref[g] = win_im.astype(BF16)
        a16r_ref[g] = jnp.exp(S5_L * lr * dt) * jnp.cos(S5_L * li * dt)
        a16i_ref[g] = jnp.exp(S5_L * lr * dt) * jnp.sin(S5_L * li * dt)

        ctr, cti = ctr_ref[g], cti_ref[g]
        z = (jnp.dot(win_re, ctr, precision=lax.Precision.HIGHEST, preferred_element_type=F32)
             - jnp.dot(win_im, cti, precision=lax.Precision.HIGHEST, preferred_element_type=F32))
        wintra = jnp.zeros((S5_BLK, S5_BLK), F32)
        for t in range(S5_L):
            sh = SSM_GROUP * (S5_L - 1 - t)
            shifted = z if sh == 0 else jnp.concatenate(
                [z[sh:, :], jnp.zeros((sh, S5_BLK), F32)], axis=0)
            wintra = jnp.where(tblk == t, shifted, wintra)
        wintra_ref[g] = wintra.astype(BF16)

        lrc, lic = lr_c[g], li_c[g]
        dtc = jnp.exp(ls_c[g])
        e_out = tcol + 1.0
        pm2 = jnp.exp(e_out * (lrc * dtc))
        ar2 = pm2 * jnp.cos(e_out * (lic * dtc))
        ai2 = pm2 * jnp.sin(e_out * (lic * dtc))
        woutr_ref[g] = (ctr * ar2 - cti * ai2).astype(BF16)
        wouti_ref[g] = (-(ctr * ai2 + cti * ar2)).astype(BF16)


def _s5_prep(lam_re, lam_im, log_step, b_re, b_im, c_re, c_im):
    n = DEPTH * SSM_GROUPS
    p = SSM_STATE
    flat = lambda a: a.reshape((n,) + a.shape[2:])
    ls = jnp.broadcast_to(log_step[..., None], (DEPTH, SSM_GROUPS, p))
    rows = [flat(a)[:, None, :] for a in (lam_re, lam_im, ls)]
    cols = [flat(a)[:, :, None] for a in (lam_re, lam_im, ls)]
    bt = [jnp.tile(jnp.swapaxes(flat(a), 1, 2), (1, S5_L, 1)) for a in (b_re, b_im)]
    ct = [jnp.tile(jnp.swapaxes(flat(a), 1, 2), (1, 1, S5_L)) for a in (c_re, c_im)]
    gb = S5_PREP_GB
    spec = lambda *tail: pl.BlockSpec((gb,) + tail, lambda i: (i,) + (0,) * len(tail))
    outs = [((S5_BLK, p), BF16), ((S5_BLK, p), BF16), ((S5_BLK, S5_BLK), BF16),
            ((p, S5_BLK), BF16), ((p, S5_BLK), BF16), ((1, p), F32), ((1, p), F32)]
    return pl.pallas_call(
        _s5_prep_kernel,
        grid=(n // gb,),
        in_specs=[spec(1, p)] * 3 + [spec(p, 1)] * 3 + [spec(S5_BLK, p)] * 2 + [spec(p, S5_BLK)] * 2,
        out_specs=[spec(*s) for s, _ in outs],
        out_shape=[jax.ShapeDtypeStruct((n,) + s, t) for s, t in outs],
        compiler_params=pltpu.CompilerParams(
            dimension_semantics=("arbitrary",), vmem_limit_bytes=VMEM_LIMIT),
        name="s5_prep",
    )(*rows, *cols, *bt, *ct)


def _s5_pair_weights(prep, d_skip):
    winr, wini, wintra, woutr, wouti, a16r, a16i = prep
    pr = lambda a: a.reshape((DEPTH, S5_PAIRS, 2) + a.shape[1:])
    winr, wini, woutr, wouti, a16r, a16i = map(pr, (winr, wini, woutr, wouti, a16r, a16i))
    z = jnp.zeros_like(winr[:, :, 0])
    wsin = jnp.concatenate([
        jnp.concatenate([winr[:, :, 0], z, wini[:, :, 0], z], axis=-1),
        jnp.concatenate([z, winr[:, :, 1], z, wini[:, :, 1]], axis=-1)], axis=-2)
    zo = jnp.zeros_like(woutr[:, :, 0])
    wout = jnp.concatenate([
        jnp.concatenate([woutr[:, :, 0], zo], axis=-1),
        jnp.concatenate([zo, woutr[:, :, 1]], axis=-1),
        jnp.concatenate([wouti[:, :, 0], zo], axis=-1),
        jnp.concatenate([zo, wouti[:, :, 1]], axis=-1)], axis=-2)
    a16 = jnp.concatenate([a16r[:, :, 0], a16r[:, :, 1], a16i[:, :, 0], a16i[:, :, 1]], axis=-1)
    dsk = jnp.tile(d_skip.reshape(DEPTH, S5_PAIRS, 2, 1, SSM_GROUP), (1, 1, 1, S5_L, 1))
    dsk = dsk.reshape(DEPTH, S5_PAIRS, 1, 2 * S5_BLK)
    wintra = wintra.reshape(DEPTH, SSM_GROUPS, S5_BLK, S5_BLK)
    return wsin, wintra, wout, a16, dsk


def _s5_kernel(u_ref, wsin_ref, wintra_ref, wout_ref, a16_ref, dsk_ref, y_ref, sin_scr, xp_scr):
    u = u_ref[...]
    sin_scr[...] = _dot(u, wsin_ref[...])
    ar = a16_ref[:, :LANES]
    ai = a16_ref[:, LANES:]

    def step(c, carry):
        xr, xi = carry
        r0 = pl.multiple_of(c * BATCH, BATCH)
        xp_scr[pl.ds(r0, BATCH), :LANES] = xr
        xp_scr[pl.ds(r0, BATCH), LANES:] = xi
        sr = sin_scr[pl.ds(r0, BATCH), :LANES]
        si = sin_scr[pl.ds(r0, BATCH), LANES:]
        return ar * xr - ai * xi + sr, ar * xi + ai * xr + si

    zero = jnp.zeros((BATCH, LANES), F32)
    lax.fori_loop(0, S5_CHUNKS, step, (zero, zero), unroll=8)

    y = _dot(xp_scr[...].astype(BF16), wout_ref[...])
    y_intra = jnp.concatenate([_dot(u[:, :S5_BLK], wintra_ref[0]),
                               _dot(u[:, S5_BLK:], wintra_ref[1])], axis=-1)
    y = y + y_intra + dsk_ref[...] * u.astype(F32)
    y_ref[...] = jax.nn.gelu(y).astype(BF16)


def _s5(u_pairs, wsin, wintra, wout, a16, dsk):
    rows = S5_CHUNKS * BATCH
    return pl.pallas_call(
        _s5_kernel,
        grid=(S5_PAIRS,),
        in_specs=[pl.BlockSpec((None, rows, 2 * S5_BLK), lambda p: (p, 0, 0)),
                  pl.BlockSpec((None, 2 * S5_BLK, 2 * LANES), lambda p: (p, 0, 0)),
                  pl.BlockSpec((2, S5_BLK, S5_BLK), lambda p: (p, 0, 0)),
                  pl.BlockSpec((None, 2 * LANES, 2 * S5_BLK), lambda p: (p, 0, 0)),
                  pl.BlockSpec((None, 1, 2 * LANES), lambda p: (p, 0, 0)),
                  pl.BlockSpec((None, 1, 2 * S5_BLK), lambda p: (p, 0, 0))],
        out_specs=pl.BlockSpec((None, rows, 2 * S5_BLK), lambda p: (p, 0, 0)),
        out_shape=jax.ShapeDtypeStruct((S5_PAIRS, rows, 2 * S5_BLK), BF16),
        scratch_shapes=[pltpu.VMEM((rows, 2 * LANES), F32), pltpu.VMEM((rows, 2 * LANES), F32)],
        compiler_params=pltpu.CompilerParams(
            dimension_semantics=("arbitrary",), vmem_limit_bytes=VMEM_LIMIT),
        name="s5",
    )(u_pairs, wsin, wintra, wout, a16, dsk)


def _mix_out_kernel(x_ref, o_ref, yg_ref, ga_ref, gs_ref, woa_ref, wglu_ref, wout_ref, gpost_ref,
                    xo_ref):
    y_att = _dot(o_ref[...], woa_ref[...])
    yg = yg_ref[...]
    y_ssm = _dot(yg, wglu_ref[:, :D_MODEL]) * jax.nn.sigmoid(_dot(yg, wglu_ref[:, D_MODEL:]))
    merged = ga_ref[...].astype(F32) * y_att + gs_ref[...].astype(F32) * y_ssm
    m = _dot(merged.astype(BF16), wout_ref[...])
    xo_ref[...] = x_ref[...] + _rms(m, gpost_ref[...])


def _mix_out(x, o, yg, ga, gs, woa, wglu, wout, gpost, tm=512):
    row = lambda d: pl.BlockSpec((None, tm, d), lambda b, i: (b, i, 0))
    full = lambda a: pl.BlockSpec(a.shape, lambda b, i: (0,) * a.ndim)
    return pl.pallas_call(
        _mix_out_kernel,
        grid=(BATCH, SEQ // tm),
        in_specs=[row(D_MODEL), row(N_HEADS * V_HEAD), row(SSM_WIDTH), row(D_MODEL), row(D_MODEL),
                  full(woa), full(wglu), full(wout), full(gpost)],
        out_specs=row(D_MODEL),
        out_shape=jax.ShapeDtypeStruct((BATCH, SEQ, D_MODEL), F32),
        compiler_params=pltpu.CompilerParams(
            dimension_semantics=("arbitrary", "arbitrary"), vmem_limit_bytes=VMEM_LIMIT),
        name="mix_out",
    )(x, o, yg, ga, gs, woa, wglu, wout, gpost)


FFN_HALO = 8


def _ffn_kernel(x_ref, gpre_ref, wup_ref, cw_ref, cb_ref, wdn_ref, gpost_ref, xo_ref, gate_scr):
    tm = x_ref.shape[0]
    i = pl.program_id(1)

    @pl.when(i == 0)
    def _():
        gate_scr[0:FFN_HALO, :] = jnp.zeros((FFN_HALO, D_FF), F32)

    @pl.when(i > 0)
    def _():
        gate_scr[0:FFN_HALO, :] = gate_scr[tm:tm + FFN_HALO, :]

    x = x_ref[...]
    h = _rms(x, gpre_ref[...]).astype(BF16)
    gate = _dot(h, wup_ref[:, :D_FF])
    val = _dot(h, wup_ref[:, D_FF:])
    gate_scr[FFN_HALO:, :] = gate
    conv = (cb_ref[...]
            + cw_ref[0:1, :] * gate_scr[pl.ds(FFN_HALO - 2, tm), :]
            + cw_ref[1:2, :] * gate_scr[pl.ds(FFN_HALO - 1, tm), :]
            + cw_ref[2:3, :] * gate)
    act = (jax.nn.gelu(conv) * val).astype(BF16)
    y = _dot(act, wdn_ref[...])
    xo_ref[...] = x + _rms(y, gpost_ref[...])


def _ffn(x, gpre, wup, cw, cb, wdn, gpost, tm=512):
    row = pl.BlockSpec((None, tm, D_MODEL), lambda b, i: (b, i, 0))
    full = lambda a: pl.BlockSpec(a.shape, lambda b, i: (0,) * a.ndim)
    return pl.pallas_call(
        _ffn_kernel,
        grid=(BATCH, SEQ // tm),
        in_specs=[row, full(gpre), full(wup), full(cw), full(cb), full(wdn), full(gpost)],
        out_specs=row,
        out_shape=jax.ShapeDtypeStruct((BATCH, SEQ, D_MODEL), F32),
        scratch_shapes=[pltpu.VMEM((tm + FFN_HALO, D_FF), F32)],
        compiler_params=pltpu.CompilerParams(
            dimension_semantics=("arbitrary", "arbitrary"), vmem_limit_bytes=VMEM_LIMIT),
        name="ffn",
    )(x, gpre, wup, cw, cb, wdn, gpost)


def _pad_cols(w, width):
    return jnp.pad(w, ((0, 0), (0, width - w.shape[1])))


def _layout_w_in(w):
    o1, o2, o3, o4 = Q_LORA, Q_LORA + KV_LORA, Q_LORA + KV_LORA + QK_ROPE, Q_LORA + KV_LORA + QK_ROPE + SSM_WIDTH
    half = QK_ROPE // 2
    kpe = w[:, o2:o3]
    kpe_sw = jnp.concatenate([kpe[:, half:], kpe[:, :half]], axis=1)
    return jnp.concatenate([w[:, :o1], w[:, o1:o2], w[:, o3:o4], w[:, o4:],
                            _pad_cols(kpe, LANES), _pad_cols(kpe_sw, LANES)], axis=1).astype(BF16)


def _layout_w_uq(w):
    half = QK_ROPE // 2
    w3 = w.reshape(Q_LORA, N_HEADS, QK_NOPE + QK_ROPE)
    pad = jnp.zeros((Q_LORA, N_HEADS, HEAD_PAD - QK_NOPE - QK_ROPE), w.dtype)
    plain = jnp.concatenate([w3, pad], axis=-1)
    swapped = jnp.concatenate([jnp.zeros((Q_LORA, N_HEADS, QK_NOPE), w.dtype),
                               w3[..., QK_NOPE + half:], w3[..., QK_NOPE:QK_NOPE + half], pad], axis=-1)
    flat = lambda a: a.reshape(Q_LORA, N_HEADS * HEAD_PAD).astype(BF16)
    return flat(plain), flat(swapped)


def _layout_w_ukv(w):
    w3 = w.reshape(KV_LORA, N_HEADS, QK_NOPE + V_HEAD)
    wk = jnp.concatenate([w3[..., :QK_NOPE],
                          jnp.zeros((KV_LORA, N_HEADS, HEAD_PAD - QK_NOPE), w.dtype)], axis=-1)
    wv = w3[..., QK_NOPE:]
    return (wk.reshape(KV_LORA, N_HEADS * HEAD_PAD).astype(BF16),
            wv.reshape(KV_LORA, N_HEADS * V_HEAD).astype(BF16))


def _rope_tables():
    pos = jnp.arange(SEQ, dtype=F32)
    inv_freq = ROPE_THETA ** (-jnp.arange(0, QK_ROPE, 2, dtype=F32) / QK_ROPE)
    ang = pos[:, None] * inv_freq[None, :]
    cos, sin = jnp.cos(ang), jnp.sin(ang)
    scale = 1.0 / math.sqrt(QK_NOPE + QK_ROPE)
    ones = jnp.ones((SEQ, QK_NOPE), F32)
    zq = jnp.zeros((SEQ, HEAD_PAD - QK_NOPE - QK_ROPE), F32)
    cq = jnp.concatenate([ones, cos, cos, zq], axis=1) * scale
    sq = jnp.concatenate([0.0 * ones, -sin, sin, zq], axis=1) * scale
    zk = jnp.zeros((SEQ, LANES - QK_ROPE), F32)
    ck = jnp.concatenate([cos, cos, zk], axis=1)
    sk = jnp.concatenate([-sin, sin, zk], axis=1)
    return cq, sq, ck, sk


def _kpe_select():
    r = jnp.arange(LANES)[:, None]
    c = jnp.arange(N_HEADS * HEAD_PAD)[None, :]
    return ((c % HEAD_PAD == r + QK_NOPE) & (r < QK_ROPE)).astype(BF16)


def _to_pairs(u):
    u6 = u.reshape(BATCH, S5_CHUNKS, S5_L, S5_PAIRS, 2, SSM_GROUP)
    return jnp.transpose(u6, (3, 1, 0, 4, 2, 5)).reshape(S5_PAIRS, S5_CHUNKS * BATCH, 2 * S5_BLK)


def _from_pairs(y):
    y6 = y.reshape(S5_PAIRS, S5_CHUNKS, BATCH, 2, S5_L, SSM_GROUP)
    return jnp.transpose(y6, (2, 1, 4, 0, 3, 5)).reshape(BATCH, SEQ, SSM_WIDTH)


def kernel(x, w_in, b_gate, g_mix_pre, g_q, w_uq, g_kv, w_ukv, w_o_att, lam_re, lam_im, log_step, b_re, b_im, c_re, c_im, d_skip, w_glu, w_out, g_mix_post, g_ffn_pre, w_up, conv_w, conv_b, w_down, g_ffn_post):
    cq, sq, ck, sk = _rope_tables()
    wksel = _kpe_select()
    prep = _s5_prep(lam_re, lam_im, log_step, b_re, b_im, c_re, c_im)
    wsin, wintra, wout_s5, a16, dsk = _s5_pair_weights(prep, d_skip)
    cw = jnp.pad(conv_w, ((0, 0), (0, FFN_HALO - CONV_W), (0, 0)))
    row = lambda a: a[None, :]
    for l in range(DEPTH):
        wuq, wuqs = _layout_w_uq(w_uq[l])
        wk, wv = _layout_w_ukv(w_ukv[l])
        q, k, v, u, ga, gs = _mix_in(
            x, row(g_mix_pre[l]), _layout_w_in(w_in[l]), row(b_gate[l, :D_MODEL]),
            row(b_gate[l, D_MODEL:]), row(g_q[l]), wuq, wuqs, row(g_kv[l]), wk, wksel, wv,
            cq, sq, ck, sk)
        o = _attn(q, k, v)
        yg = _s5(_to_pairs(u), wsin[l], wintra[l], wout_s5[l], a16[l], dsk[l])
        x = _mix_out(x, o, _from_pairs(yg), ga, gs, w_o_att[l].astype(BF16), w_glu[l].astype(BF16),
                     w_out[l].astype(BF16), row(g_mix_post[l]))
        x = _ffn(x, row(g_ffn_pre[l]), w_up[l].astype(BF16), cw[l], row(conv_b[l]),
                 w_down[l].astype(BF16), row(g_ffn_post[l]))
    return x
```

```python
import functools
import math

import jax
import jax.numpy as jnp
from jax import lax
from jax.experimental import pallas as pl
from jax.experimental.pallas import tpu as pltpu

D_MODEL = 1024
BATCH = 16
SEQ = 2048
DEPTH = 4
CHUNK = 64
N_HEADS = 8
QK_NOPE = 64
QK_ROPE = 32
V_HEAD = 64
Q_LORA = 384
KV_LORA = 256
ROPE_THETA = 10000.0
SSM_WIDTH = 512
SSM_GROUP = 16
SSM_GROUPS = SSM_WIDTH // SSM_GROUP
SSM_STATE = 64
D_FF = 2816
CONV_W = 3
EPS = 1e-6

LANES = 128
HEAD_PAD = LANES
S5_L = 16
S5_CHUNKS = SEQ // S5_L
S5_BLK = S5_L * SSM_GROUP
S5_GL = LANES // SSM_GROUP
S5_NB = SSM_GROUPS // S5_GL
S5_IN = S5_L * LANES
S5_ST = S5_GL * SSM_STATE
VMEM_LIMIT = 56 * 1024 * 1024

F32 = jnp.float32
BF16 = jnp.bfloat16


def _rms(x, g):
    var = jnp.mean(x * x, axis=-1, keepdims=True)
    return x * lax.rsqrt(var + EPS) * g


def _dot(a, b):
    return jnp.dot(a, b, preferred_element_type=F32)


def _dot_nt(a, b):
    return lax.dot_general(a, b, (((1,), (1,)), ((), ())), preferred_element_type=F32)


C_Q0, C_KV0, C_U0, C_GA0, C_GS0, C_KA0, C_KB0, C_END = 0, 384, 640, 1152, 2176, 3200, 3328, 3456


def _mix_in_kernel(x_ref, gpre_ref, win_ref, bga_ref, bgs_ref, gq_ref, wuq_ref, wuqs_ref,
                   gkv_ref, wk_ref, wksel_ref, wv_ref, cq_ref, sq_ref, ck_ref, sk_ref,
                   q_ref, k_ref, v_ref, u_ref, ga_ref, gs_ref):
    h = _rms(x_ref[...], gpre_ref[...]).astype(BF16)

    c_q = _dot(h, win_ref[:, C_Q0:C_KV0])
    cqn = _rms(c_q, gq_ref[...]).astype(BF16)
    q = _dot(cqn, wuq_ref[...])
    qs = _dot(cqn, wuqs_ref[...])
    cq, sq = cq_ref[...], sq_ref[...]
    for hd in range(N_HEADS):
        sl = slice(hd * HEAD_PAD, (hd + 1) * HEAD_PAD)
        q_ref[:, sl] = (q[:, sl] * cq + qs[:, sl] * sq).astype(BF16)

    c_kv = _dot(h, win_ref[:, C_KV0:C_U0])
    ckvn = _rms(c_kv, gkv_ref[...]).astype(BF16)
    kpe = (_dot(h, win_ref[:, C_KA0:C_KB0]) * ck_ref[...]
           + _dot(h, win_ref[:, C_KB0:C_END]) * sk_ref[...]).astype(BF16)
    k_ref[...] = (_dot(ckvn, wk_ref[...]) + _dot(kpe, wksel_ref[...])).astype(BF16)
    v_ref[...] = _dot(ckvn, wv_ref[...]).astype(BF16)

    u = _dot(h, win_ref[:, C_U0:C_GA0])
    for gb in range(S5_NB):
        ub = u[:, gb * LANES:(gb + 1) * LANES].reshape(-1, S5_L, LANES)
        u_ref[gb] = ub.astype(BF16)
    ga_ref[...] = jax.nn.sigmoid(_dot(h, win_ref[:, C_GA0:C_GS0]) + bga_ref[...]).astype(BF16)
    gs_ref[...] = jax.nn.sigmoid(_dot(h, win_ref[:, C_GS0:C_KA0]) + bgs_ref[...]).astype(BF16)


def _mix_in(x, gpre, win, bga, bgs, gq, wuq, wuqs, gkv, wk, wksel, wv, cq, sq, ck, sk, tm=512):
    nt = SEQ // tm
    row = lambda d: pl.BlockSpec((None, tm, d), lambda i, b: (b, i, 0))
    full = lambda a: pl.BlockSpec(a.shape, lambda i, b: (0,) * a.ndim)
    tab = pl.BlockSpec((tm, LANES), lambda i, b: (i, 0))
    hq = N_HEADS * HEAD_PAD
    outs = [(hq, BF16), (hq, BF16), (N_HEADS * V_HEAD, BF16), None, (D_MODEL, BF16), (D_MODEL, BF16)]
    u_spec = pl.BlockSpec((S5_NB, tm // S5_L, None, S5_L, LANES), lambda i, b: (0, i, b, 0, 0))
    u_shape = jax.ShapeDtypeStruct((S5_NB, S5_CHUNKS, BATCH, S5_L, LANES), BF16)
    return pl.pallas_call(
        _mix_in_kernel,
        grid=(nt, BATCH),
        in_specs=[row(D_MODEL), full(gpre), full(win), full(bga), full(bgs), full(gq), full(wuq),
                  full(wuqs), full(gkv), full(wk), full(wksel), full(wv), tab, tab, tab, tab],
        out_specs=[u_spec if o is None else row(o[0]) for o in outs],
        out_shape=[u_shape if o is None else jax.ShapeDtypeStruct((BATCH, SEQ, o[0]), o[1])
                   for o in outs],
        compiler_params=pltpu.CompilerParams(
            dimension_semantics=("arbitrary", "arbitrary"), vmem_limit_bytes=VMEM_LIMIT),
        name="mix_in",
    )(x, gpre, win, bga, bgs, gq, wuq, wuqs, gkv, wk, wksel, wv, cq, sq, ck, sk)


ATT_TQ = 256


def _attn_kernel(q_ref, k_ref, v_ref, o_ref):
    tq = ATT_TQ
    row_chunk = lax.broadcasted_iota(jnp.int32, (tq, tq), 0) // CHUNK
    col_chunk = lax.broadcasted_iota(jnp.int32, (tq, tq), 1) // CHUNK
    mask = row_chunk >= col_chunk
    lane = lax.broadcasted_iota(jnp.int32, (tq, 2 * V_HEAD), 1)
    for i in range(SEQ // tq):
        q0 = i * tq
        outs = []
        for hh in range(2):
            hs = slice(hh * HEAD_PAD, (hh + 1) * HEAD_PAD)
            qh = q_ref[q0:q0 + tq, hs]
            s_d = jnp.where(mask, _dot_nt(qh, k_ref[q0:q0 + tq, hs]), -jnp.inf)
            m = jnp.max(s_d, axis=-1, keepdims=True)
            if i > 0:
                s_p = _dot_nt(qh, k_ref[0:q0, hs])
                m = jnp.maximum(m, jnp.max(s_p, axis=-1, keepdims=True))
            p_d = jnp.exp2(s_d - m)
            l = jnp.sum(p_d, axis=-1, keepdims=True)
            acc = _dot(p_d.astype(BF16), v_ref[q0:q0 + tq, :])
            if i > 0:
                p_p = jnp.exp2(s_p - m)
                l = l + jnp.sum(p_p, axis=-1, keepdims=True)
                acc = acc + _dot(p_p.astype(BF16), v_ref[0:q0, :])
            outs.append(acc / l)
        o_ref[q0:q0 + tq, :] = jnp.where(lane < V_HEAD, outs[0], outs[1]).astype(BF16)


def _attn(q, k, v):
    blk = lambda d: pl.BlockSpec((None, SEQ, d), lambda b, p: (b, 0, p))
    return pl.pallas_call(
        _attn_kernel,
        grid=(BATCH, N_HEADS // 2),
        in_specs=[blk(2 * HEAD_PAD), blk(2 * HEAD_PAD), blk(2 * V_HEAD)],
        out_specs=blk(2 * V_HEAD),
        out_shape=jax.ShapeDtypeStruct((BATCH, SEQ, N_HEADS * V_HEAD), BF16),
        compiler_params=pltpu.CompilerParams(
            dimension_semantics=("arbitrary", "arbitrary"), vmem_limit_bytes=VMEM_LIMIT),
        name="attn",
    )(q, k, v)


S5_PREP_GB = 8


def _s5_prep_kernel(lr_r, li_r, ls_r, lr_c, li_c, ls_c, btr_ref, bti_ref, ctr_ref, cti_ref,
                    winr_ref, wini_ref, wintra_ref, woutr_ref, wouti_ref, a16r_ref, a16i_ref):
    srow = (lax.broadcasted_iota(jnp.int32, (S5_BLK, 1), 0) // SSM_GROUP).astype(F32)
    tcol = (lax.broadcasted_iota(jnp.int32, (1, S5_BLK), 1) // SSM_GROUP).astype(F32)
    tblk = lax.broadcasted_iota(jnp.int32, (S5_BLK, S5_BLK), 1) // SSM_GROUP
    for g in range(S5_PREP_GB):
        lr, li = lr_r[g], li_r[g]
        dt = jnp.exp(ls_r[g])
        mag = jnp.exp(lr * dt)
        abar_re = mag * jnp.cos(li * dt)
        abar_im = mag * jnp.sin(li * dt)
        nr, ni = abar_re - 1.0, abar_im
        den = lr * lr + li * li
        f_re = (nr * lr + ni * li) / den
        f_im = (ni * lr - nr * li) / den
        btr, bti = btr_ref[g], bti_ref[g]
        bbar_re = f_re * btr - f_im * bti
        bbar_im = f_re * bti + f_im * btr
        e_in = (S5_L - 1.0) - srow
        pm = jnp.exp(e_in * (lr * dt))
        pr = pm * jnp.cos(e_in * (li * dt))
        pi = pm * jnp.sin(e_in * (li * dt))
        win_re = bbar_re * pr - bbar_im * pi
        win_im = bbar_re * pi + bbar_im * pr
        winr_ref[g] = win_re.astype(BF16)
        wini_ref[g] = win_im.astype(BF16)
        a16r_ref[g] = jnp.exp(S5_L * lr * dt) * jnp.cos(S5_L * li * dt)
        a16i_ref[g] = jnp.exp(S5_L * lr * dt) * jnp.sin(S5_L * li * dt)

        ctr, cti = ctr_ref[g], cti_ref[g]
        z = (jnp.dot(win_re, ctr, precision=lax.Precision.HIGHEST, preferred_element_type=F32)
             - jnp.dot(win_im, cti, precision=lax.Precision.HIGHEST, preferred_element_type=F32))
        wintra = jnp.zeros((S5_BLK, S5_BLK), F32)
        for t in range(S5_L):
            sh = SSM_GROUP * (S5_L - 1 - t)
            shifted = z if sh == 0 else jnp.concatenate(
                [z[sh:, :], jnp.zeros((sh, S5_BLK), F32)], axis=0)
            wintra = jnp.where(tblk == t, shifted, wintra)
        wintra_ref[g] = wintra.astype(BF16)

        lrc, lic = lr_c[g], li_c[g]
        dtc = jnp.exp(ls_c[g])
        e_out = tcol + 1.0
        pm2 = jnp.exp(e_out * (lrc * dtc))
        ar2 = pm2 * jnp.cos(e_out * (lic * dtc))
        ai2 = pm2 * jnp.sin(e_out * (lic * dtc))
        woutr_ref[g] = (ctr * ar2 - cti * ai2).astype(BF16)
        wouti_ref[g] = (-(ctr * ai2 + cti * ar2)).astype(BF16)


def _s5_prep(lam_re, lam_im, log_step, b_re, b_im, c_re, c_im):
    n = DEPTH * SSM_GROUPS
    p = SSM_STATE
    flat = lambda a: a.reshape((n,) + a.shape[2:])
    ls = jnp.broadcast_to(log_step[..., None], (DEPTH, SSM_GROUPS, p))
    rows = [flat(a)[:, None, :] for a in (lam_re, lam_im, ls)]
    cols = [flat(a)[:, :, None] for a in (lam_re, lam_im, ls)]
    bt = [jnp.tile(jnp.swapaxes(flat(a), 1, 2), (1, S5_L, 1)) for a in (b_re, b_im)]
    ct = [jnp.tile(jnp.swapaxes(flat(a), 1, 2), (1, 1, S5_L)) for a in (c_re, c_im)]
    gb = S5_PREP_GB
    spec = lambda *tail: pl.BlockSpec((gb,) + tail, lambda i: (i,) + (0,) * len(tail))
    outs = [((S5_BLK, p), BF16), ((S5_BLK, p), BF16), ((S5_BLK, S5_BLK), BF16),
            ((p, S5_BLK), BF16), ((p, S5_BLK), BF16), ((1, p), F32), ((1, p), F32)]
    return pl.pallas_call(
        _s5_prep_kernel,
        grid=(n // gb,),
        in_specs=[spec(1, p)] * 3 + [spec(p, 1)] * 3 + [spec(S5_BLK, p)] * 2 + [spec(p, S5_BLK)] * 2,
        out_specs=[spec(*s) for s, _ in outs],
        out_shape=[jax.ShapeDtypeStruct((n,) + s, t) for s, t in outs],
        compiler_params=pltpu.CompilerParams(
            dimension_semantics=("arbitrary",), vmem_limit_bytes=VMEM_LIMIT),
        name="s5_prep",
    )(*rows, *cols, *bt, *ct)


def _s5_block_weights(prep, d_skip):
    winr, wini, wintra, woutr, wouti, a16r, a16i = prep
    head = (DEPTH, S5_NB, S5_GL)
    eye = jnp.eye(S5_GL, dtype=bool)[None, None, None, :, None, None, :, None]
    win = jnp.stack([winr, wini], axis=1).reshape(head + (2, S5_L, SSM_GROUP, SSM_STATE))
    win = jnp.transpose(win, (0, 1, 4, 2, 5, 3, 6))[..., None, :]
    wsin = jnp.where(eye, win, 0).reshape(DEPTH, S5_NB, S5_IN, 2 * S5_ST)
    wi = wintra.reshape(head + (S5_L, SSM_GROUP, S5_L, SSM_GROUP))
    wi = jnp.transpose(wi, (0, 1, 3, 2, 4, 5, 6))[..., None, :]
    wintra = jnp.where(eye, wi, 0).reshape(DEPTH, S5_NB, S5_IN, S5_IN)
    wo = jnp.stack([woutr, wouti], axis=1).reshape(head + (2, SSM_STATE, S5_L, SSM_GROUP))
    wo = jnp.transpose(wo, (0, 1, 3, 2, 4, 5, 6))[..., None, :]
    wout = jnp.where(eye, wo, 0).reshape(DEPTH, S5_NB, 2 * S5_ST, S5_IN)
    a16 = jnp.stack([a16r, a16i], axis=1).reshape(head + (2, SSM_STATE))
    a16 = jnp.transpose(a16, (0, 1, 3, 2, 4)).reshape(DEPTH, S5_NB, 1, 2 * S5_ST)
    dsk = jnp.tile(d_skip.reshape(DEPTH, S5_NB, 1, LANES), (1, 1, 1, S5_L))
    return wsin, wintra, wout, a16, dsk


S5_CT = 32
S5_TB = 2 * LANES


def _s5_kernel(u_ref, wsin_ref, wintra_ref, wout_ref, a16_ref, dsk_ref, y_ref,
               sin_scr, xp_scr, carry_scr):
    @pl.when(pl.program_id(1) == 0)
    def _():
        carry_scr[...] = jnp.zeros_like(carry_scr)

    sin_scr[...] = _dot(u_ref[...], wsin_ref[...])
    ar = a16_ref[:, :S5_ST]
    ai = a16_ref[:, S5_ST:]

    def step(c, carry):
        xr, xi = carry
        r0 = pl.multiple_of(c * BATCH, BATCH)
        xp_scr[pl.ds(r0, BATCH), :S5_ST] = xr
        xp_scr[pl.ds(r0, BATCH), S5_ST:] = xi
        sr = sin_scr[pl.ds(r0, BATCH), :S5_ST]
        si = sin_scr[pl.ds(r0, BATCH), S5_ST:]
        return ar * xr - ai * xi + sr, ar * xi + ai * xr + si

    xr, xi = lax.fori_loop(0, S5_CT, step, (carry_scr[:, :S5_ST], carry_scr[:, S5_ST:]), unroll=4)
    carry_scr[:, :S5_ST] = xr
    carry_scr[:, S5_ST:] = xi

    xp = xp_scr[...].astype(BF16)
    for j in range(S5_IN // S5_TB):
        cols = slice(j * S5_TB, (j + 1) * S5_TB)
        kend = (j + 1) * S5_TB
        y = (_dot(xp, wout_ref[:, cols]) + _dot(u_ref[:, :kend], wintra_ref[:kend, cols])
             + dsk_ref[:, cols] * u_ref[:, cols].astype(F32))
        y_ref[:, cols] = jax.nn.gelu(y).astype(BF16)


def _s5(u, layer, wsin, wintra, wout, a16, dsk):
    rows = S5_CT * BATCH
    wspec = lambda a: pl.BlockSpec((None, None) + a.shape[2:], lambda g, c: (layer, g, 0, 0))
    xspec = pl.BlockSpec((None, rows, S5_IN), lambda g, c: (g, c, 0))
    u2 = u.reshape(S5_NB, S5_CHUNKS * BATCH, S5_IN)
    y = pl.pallas_call(
        _s5_kernel,
        grid=(S5_NB, S5_CHUNKS // S5_CT),
        in_specs=[xspec, wspec(wsin), wspec(wintra), wspec(wout), wspec(a16), wspec(dsk)],
        out_specs=xspec,
        out_shape=jax.ShapeDtypeStruct(u2.shape, BF16),
        scratch_shapes=[pltpu.VMEM((rows, 2 * S5_ST), F32), pltpu.VMEM((rows, 2 * S5_ST), F32),
                        pltpu.VMEM((BATCH, 2 * S5_ST), F32)],
        compiler_params=pltpu.CompilerParams(
            dimension_semantics=("arbitrary", "arbitrary"), vmem_limit_bytes=VMEM_LIMIT),
        name="s5",
    )(u2, wsin, wintra, wout, a16, dsk)
    return y.reshape(u.shape)


def _mix_out_kernel(x_ref, o_ref, yg_ref, ga_ref, gs_ref, woa_ref, wglu_ref, wout_ref, gpost_ref,
                    xo_ref):
    y_att = _dot(o_ref[...], woa_ref[...])
    yg = jnp.concatenate([yg_ref[gb].reshape(-1, LANES) for gb in range(S5_NB)], axis=-1)
    y_ssm =_dot(yg, wglu_ref[:, :D_MODEL]) * jax.nn.sigmoid(_dot(yg, wglu_ref[:, D_MODEL:]))
    merged = ga_ref[...].astype(F32) * y_att + gs_ref[...].astype(F32) * y_ssm
    m = _dot(merged.astype(BF16), wout_ref[...])
    xo_ref[...] = x_ref[...] + _rms(m, gpost_ref[...])


def _mix_out(x, o, yg, ga, gs, woa, wglu, wout, gpost, tm=512):
    row = lambda d: pl.BlockSpec((None, tm, d), lambda b, i: (b, i, 0))
    full = lambda a: pl.BlockSpec(a.shape, lambda b, i: (0,) * a.ndim)
    yg_spec = pl.BlockSpec((S5_NB, tm // S5_L, None, S5_L, LANES), lambda b, i: (0, i, b, 0, 0))
    return pl.pallas_call(
        _mix_out_kernel,
        grid=(BATCH, SEQ // tm),
        in_specs=[row(D_MODEL), row(N_HEADS * V_HEAD), yg_spec, row(D_MODEL), row(D_MODEL),
                  full(woa), full(wglu), full(wout), full(gpost)],
        out_specs=row(D_MODEL),
        out_shape=jax.ShapeDtypeStruct((BATCH, SEQ, D_MODEL), F32),
        compiler_params=pltpu.CompilerParams(
            dimension_semantics=("arbitrary", "arbitrary"), vmem_limit_bytes=VMEM_LIMIT),
        name="mix_out",
    )(x, o, yg, ga, gs, woa, wglu, wout, gpost)


FFN_HALO = 8


def _ffn_kernel(x_ref, gpre_ref, wup_ref, cw_ref, cb_ref, wdn_ref, gpost_ref, xo_ref, gate_scr):
    tm = x_ref.shape[0]
    i = pl.program_id(1)

    @pl.when(i == 0)
    def _():
        gate_scr[0:FFN_HALO, :] = jnp.zeros((FFN_HALO, D_FF), F32)

    @pl.when(i > 0)
    def _():
        gate_scr[0:FFN_HALO, :] = gate_scr[tm:tm + FFN_HALO, :]

    x = x_ref[...]
    h = _rms(x, gpre_ref[...]).astype(BF16)
    gate = _dot(h, wup_ref[:, :D_FF])
    val = _dot(h, wup_ref[:, D_FF:])
    gate_scr[FFN_HALO:, :] = gate
    conv = (cb_ref[...]
            + cw_ref[0:1, :] * gate_scr[pl.ds(FFN_HALO - 2, tm), :]
            + cw_ref[1:2, :] * gate_scr[pl.ds(FFN_HALO - 1, tm), :]
            + cw_ref[2:3, :] * gate)
    act = (jax.nn.gelu(conv) * val).astype(BF16)
    y = _dot(act, wdn_ref[...])
    xo_ref[...] = x + _rms(y, gpost_ref[...])


def _ffn(x, gpre, wup, cw, cb, wdn, gpost, tm=512):
    row = pl.BlockSpec((None, tm, D_MODEL), lambda b, i: (b, i, 0))
    full = lambda a: pl.BlockSpec(a.shape, lambda b, i: (0,) * a.ndim)
    return pl.pallas_call(
        _ffn_kernel,
        grid=(BATCH, SEQ // tm),
        in_specs=[row, full(gpre), full(wup), full(cw), full(cb), full(wdn), full(gpost)],
        out_specs=row,
        out_shape=jax.ShapeDtypeStruct((BATCH, SEQ, D_MODEL), F32),
        scratch_shapes=[pltpu.VMEM((tm + FFN_HALO, D_FF), F32)],
        compiler_params=pltpu.CompilerParams(
            dimension_semantics=("arbitrary", "arbitrary"), vmem_limit_bytes=VMEM_LIMIT),
        name="ffn",
    )(x, gpre, wup, cw, cb, wdn, gpost)


def _pad_cols(w, width):
    return jnp.pad(w, ((0, 0), (0, width - w.shape[1])))


def _layout_w_in(w):
    o1, o2, o3, o4 = Q_LORA, Q_LORA + KV_LORA, Q_LORA + KV_LORA + QK_ROPE, Q_LORA + KV_LORA + QK_ROPE + SSM_WIDTH
    half = QK_ROPE // 2
    kpe = w[:, o2:o3]
    kpe_sw = jnp.concatenate([kpe[:, half:], kpe[:, :half]], axis=1)
    return jnp.concatenate([w[:, :o1], w[:, o1:o2], w[:, o3:o4], w[:, o4:],
                            _pad_cols(kpe, LANES), _pad_cols(kpe_sw, LANES)], axis=1).astype(BF16)


def _layout_w_uq(w):
    half = QK_ROPE // 2
    w3 = w.reshape(Q_LORA, N_HEADS, QK_NOPE + QK_ROPE)
    pad = jnp.zeros((Q_LORA, N_HEADS, HEAD_PAD - QK_NOPE - QK_ROPE), w.dtype)
    plain = jnp.concatenate([w3, pad], axis=-1)
    swapped = jnp.concatenate([jnp.zeros((Q_LORA, N_HEADS, QK_NOPE), w.dtype),
                               w3[..., QK_NOPE + half:], w3[..., QK_NOPE:QK_NOPE + half], pad], axis=-1)
    flat = lambda a: a.reshape(Q_LORA, N_HEADS * HEAD_PAD).astype(BF16)
    return flat(plain), flat(swapped)


def _layout_w_ukv(w):
    w3 = w.reshape(KV_LORA, N_HEADS, QK_NOPE + V_HEAD)
    wk = jnp.concatenate([w3[..., :QK_NOPE],
                          jnp.zeros((KV_LORA, N_HEADS, HEAD_PAD - QK_NOPE), w.dtype)], axis=-1)
    wv = w3[..., QK_NOPE:]
    return (wk.reshape(KV_LORA, N_HEADS * HEAD_PAD).astype(BF16),
            wv.reshape(KV_LORA, N_HEADS * V_HEAD).astype(BF16))


def _rope_tables():
    pos = jnp.arange(SEQ, dtype=F32)
    inv_freq = ROPE_THETA ** (-jnp.arange(0, QK_ROPE, 2, dtype=F32) / QK_ROPE)
    ang = pos[:, None] * inv_freq[None, :]
    cos, sin = jnp.cos(ang), jnp.sin(ang)
    scale = math.log2(math.e) / math.sqrt(QK_NOPE + QK_ROPE)
    ones = jnp.ones((SEQ, QK_NOPE), F32)
    zq = jnp.zeros((SEQ, HEAD_PAD - QK_NOPE - QK_ROPE), F32)
    cq = jnp.concatenate([ones, cos, cos, zq], axis=1) * scale
    sq = jnp.concatenate([0.0 * ones, -sin, sin, zq], axis=1) * scale
    zk = jnp.zeros((SEQ, LANES - QK_ROPE), F32)
    ck = jnp.concatenate([cos, cos, zk], axis=1)
    sk = jnp.concatenate([-sin, sin, zk], axis=1)
    return cq, sq, ck, sk


def _kpe_select():
    r = jnp.arange(LANES)[:, None]
    c = jnp.arange(N_HEADS * HEAD_PAD)[None, :]
    return ((c % HEAD_PAD == r + QK_NOPE) & (r < QK_ROPE)).astype(BF16)


def kernel(x, w_in, b_gate, g_mix_pre, g_q, w_uq, g_kv, w_ukv, w_o_att, lam_re, lam_im, log_step, b_re, b_im, c_re, c_im, d_skip, w_glu, w_out, g_mix_post, g_ffn_pre, w_up, conv_w, conv_b, w_down, g_ffn_post):
    cq, sq, ck, sk = _rope_tables()
    wksel = _kpe_select()
    prep = _s5_prep(lam_re, lam_im, log_step, b_re, b_im, c_re, c_im)
    s5w = _s5_block_weights(prep, d_skip)
    cw = jnp.pad(conv_w, ((0, 0), (0, FFN_HALO - CONV_W), (0, 0)))
    row = lambda a: a[None, :]
    for l in range(DEPTH):
        wuq, wuqs = _layout_w_uq(w_uq[l])
        wk, wv = _layout_w_ukv(w_ukv[l])
        q, k, v, u, ga, gs = _mix_in(
            x, row(g_mix_pre[l]), _layout_w_in(w_in[l]), row(b_gate[l, :D_MODEL]),
            row(b_gate[l, D_MODEL:]), row(g_q[l]), wuq, wuqs, row(g_kv[l]), wk, wksel, wv,
            cq, sq, ck, sk)
        o = _attn(q, k, v)
        yg = _s5(u, l, *s5w)
        x = _mix_out(x, o, yg, ga, gs, w_o_att[l].astype(BF16), w_glu[l].astype(BF16),
                     w_out[l].astype(BF16), row(g_mix_post[l]))
        x = _ffn(x, row(g_ffn_pre[l]), w_up[l].astype(BF16), cw[l], row(conv_b[l]),
                 w_down[l].astype(BF16), row(g_ffn_post[l]))
    return x
```

```python
import math

import jax
import jax.numpy as jnp
from jax import lax
from jax.experimental import pallas as pl
from jax.experimental.pallas import tpu as pltpu

D_MODEL = 1024
BATCH = 16
SEQ = 2048
DEPTH = 4
CHUNK = 64
N_HEADS = 8
QK_NOPE = 64
QK_ROPE = 32
V_HEAD = 64
Q_LORA = 384
KV_LORA = 256
ROPE_THETA = 10000.0
SSM_WIDTH = 512
SSM_GROUP = 16
SSM_GROUPS = SSM_WIDTH // SSM_GROUP
SSM_STATE = 64
D_FF = 2816
CONV_W = 3
EPS = 1e-6

LANES = 128
HEAD_PAD = LANES
S5_L = 16
S5_CHUNKS = SEQ // S5_L
S5_GL = LANES // SSM_GROUP
S5_NB = SSM_GROUPS // S5_GL
S5_IN = S5_L * LANES
S5_ST = S5_GL * SSM_STATE
VMEM_LIMIT = 56 * 1024 * 1024

F32 = jnp.float32
BF16 = jnp.bfloat16


def _rms(x, g):
    var = jnp.mean(x * x, axis=-1, keepdims=True)
    return x * lax.rsqrt(var + EPS) * g


def _dot(a, b):
    return jnp.dot(a, b, preferred_element_type=F32)


def _dot_nt(a, b):
    return lax.dot_general(a, b, (((1,), (1,)), ((), ())), preferred_element_type=F32)


C_Q0, C_KPE0, C_KV0, C_U0, C_GA0, C_GS0, C_END = 0, 384, 512, 768, 1280, 2304, 3328
ROPE_HALF = QK_ROPE // 2


def _swap_halves(x, lo):
    lane = lax.broadcasted_iota(jnp.int32, x.shape, 1)
    first = (lane >= lo) & (lane < lo + ROPE_HALF)
    return jnp.where(first, pltpu.roll(x, LANES - ROPE_HALF, axis=1), pltpu.roll(x, ROPE_HALF, axis=1))


def _mix_in_kernel(x_ref, gpre_ref, win_ref, bga_ref, bgs_ref, gq_ref, wuq_ref,
                   gkv_ref, wkn_ref, wv_ref, cq_ref, sq_ref, ck_ref, sk_ref,
                   q_ref, k_ref, v_ref, u_ref, ga_ref, gs_ref):
    h = _rms(x_ref[...], gpre_ref[...]).astype(BF16)

    z_q = _dot(h, win_ref[:, C_Q0:C_KV0])
    cqn = _rms(z_q[:, :Q_LORA], gq_ref[...]).astype(BF16)
    q = _dot(cqn, wuq_ref[...])
    cq, sq = cq_ref[...], sq_ref[...]
    for hd in range(N_HEADS):
        sl = slice(hd * HEAD_PAD, (hd + 1) * HEAD_PAD)
        qh = q[:, sl]
        q_ref[:, sl] = (qh * cq + _swap_halves(qh, QK_NOPE) * sq).astype(BF16)

    kpe = z_q[:, C_KPE0:C_KV0]
    kpe = pltpu.roll(kpe * ck_ref[...] + _swap_halves(kpe, 0) * sk_ref[...], QK_NOPE, axis=1)

    c_kv = _dot(h, win_ref[:, C_KV0:C_U0])
    ckvn = _rms(c_kv, gkv_ref[...]).astype(BF16)
    kn = _dot(ckvn, wkn_ref[...])
    lane = lax.broadcasted_iota(jnp.int32, kpe.shape, 1)
    for hp in range(N_HEADS // 2):
        pair = kn[:, hp * LANES:(hp + 1) * LANES]
        for hh, nope in enumerate((pair, pltpu.roll(pair, QK_NOPE, axis=1))):
            sl = slice((2 * hp + hh) * HEAD_PAD, (2 * hp + hh + 1) * HEAD_PAD)
            k_ref[:, sl] = jnp.where(lane < QK_NOPE, nope, kpe).astype(BF16)
    v_ref[...] = _dot(ckvn, wv_ref[...]).astype(BF16)

    u = _dot(h, win_ref[:, C_U0:C_GA0])
    for gb in range(S5_NB):
        ub = u[:, gb * LANES:(gb + 1) * LANES].reshape(-1, S5_L, LANES)
        u_ref[gb] = ub.astype(BF16)
    ga_ref[...] = jax.nn.sigmoid(_dot(h, win_ref[:, C_GA0:C_GS0]) + bga_ref[...]).astype(BF16)
    gs_ref[...] = jax.nn.sigmoid(_dot(h, win_ref[:, C_GS0:C_END]) + bgs_ref[...]).astype(BF16)


def _mix_in(x, gpre, win, bga, bgs, gq, wuq, gkv, wkn, wv, cq, sq, ck, sk, tm=512):
    nt = SEQ // tm
    row = lambda d: pl.BlockSpec((None, tm, d), lambda i, b: (b, i, 0))
    full = lambda a: pl.BlockSpec(a.shape, lambda i, b: (0,) * a.ndim)
    tab = pl.BlockSpec((tm, LANES), lambda i, b: (i, 0))
    hq = N_HEADS * HEAD_PAD
    outs = [(hq, BF16), (hq, BF16), (N_HEADS * V_HEAD, BF16), None, (D_MODEL, BF16), (D_MODEL, BF16)]
    u_spec = pl.BlockSpec((S5_NB, tm // S5_L, None, S5_L, LANES), lambda i, b: (0, i, b, 0, 0))
    u_shape = jax.ShapeDtypeStruct((S5_NB, S5_CHUNKS, BATCH, S5_L, LANES), BF16)
    return pl.pallas_call(
        _mix_in_kernel,
        grid=(nt, BATCH),
        in_specs=[row(D_MODEL), full(gpre), full(win), full(bga), full(bgs), full(gq), full(wuq),
                  full(gkv), full(wkn), full(wv), tab, tab, tab, tab],
        out_specs=[u_spec if o is None else row(o[0]) for o in outs],
        out_shape=[u_shape if o is None else jax.ShapeDtypeStruct((BATCH, SEQ, o[0]), o[1])
                   for o in outs],
        compiler_params=pltpu.CompilerParams(
            dimension_semantics=("arbitrary", "arbitrary"), vmem_limit_bytes=VMEM_LIMIT),
        name="mix_in",
    )(x, gpre, win, bga, bgs, gq, wuq, gkv, wkn, wv, cq, sq, ck, sk)


ATT_TQ = 256


def _attn_kernel(q_ref, k_ref, v_ref, o_ref):
    tq = ATT_TQ
    row_chunk = lax.broadcasted_iota(jnp.int32, (tq, tq), 0) // CHUNK
    col_chunk = lax.broadcasted_iota(jnp.int32, (tq, tq), 1) // CHUNK
    mask = row_chunk >= col_chunk
    lane = lax.broadcasted_iota(jnp.int32, (tq, 2 * V_HEAD), 1)
    for i in range(SEQ // tq):
        q0 = i * tq
        outs = []
        for hh in range(2):
            hs = slice(hh * HEAD_PAD, (hh + 1) * HEAD_PAD)
            qh = q_ref[q0:q0 + tq, hs]
            s_d = jnp.where(mask, _dot_nt(qh, k_ref[q0:q0 + tq, hs]), -jnp.inf)
            m = jnp.max(s_d, axis=-1, keepdims=True)
            if i > 0:
                s_p = _dot_nt(qh, k_ref[0:q0, hs])
                m = jnp.maximum(m, jnp.max(s_p, axis=-1, keepdims=True))
            p_d = jnp.exp2(s_d - m)
            l = jnp.sum(p_d, axis=-1, keepdims=True)
            acc = _dot(p_d.astype(BF16), v_ref[q0:q0 + tq, :])
            if i > 0:
                p_p = jnp.exp2(s_p - m)
                l = l + jnp.sum(p_p, axis=-1, keepdims=True)
                acc = acc + _dot(p_p.astype(BF16), v_ref[0:q0, :])
            outs.append(acc / l)
        o_ref[q0:q0 + tq, :] = jnp.where(lane < V_HEAD, outs[0], outs[1]).astype(BF16)


def _attn(q, k, v):
    blk = lambda d: pl.BlockSpec((None, SEQ, d), lambda b, p: (b, 0, p))
    return pl.pallas_call(
        _attn_kernel,
        grid=(BATCH, N_HEADS // 2),
        in_specs=[blk(2 * HEAD_PAD), blk(2 * HEAD_PAD), blk(2 * V_HEAD)],
        out_specs=blk(2 * V_HEAD),
        out_shape=jax.ShapeDtypeStruct((BATCH, SEQ, N_HEADS * V_HEAD), BF16),
        compiler_params=pltpu.CompilerParams(
            dimension_semantics=("arbitrary", "arbitrary"), vmem_limit_bytes=VMEM_LIMIT),
        name="attn",
    )(q, k, v)


def _s5_prep_kernel(lr_r, li_r, ls_r, lr_c, li_c, ls_c, b16r_ref, b16i_ref, cxr_ref, cxi_ref,
                    wsin_ref, wout_ref, z_ref, a16_ref, winr_scr, wini_scr):
    lr, li = lr_r[...], li_r[...]
    dt = jnp.exp(ls_r[...])
    mag = jnp.exp(lr * dt)
    abar_re = mag * jnp.cos(li * dt)
    abar_im = mag * jnp.sin(li * dt)
    nr, ni = abar_re - 1.0, abar_im
    den = lr * lr + li * li
    f_re = (nr * lr + ni * li) / den
    f_im = (ni * lr - nr * li) / den
    b16r, b16i = b16r_ref[...], b16i_ref[...]
    bbar_re = f_re * b16r - f_im * b16i
    bbar_im = f_re * b16i + f_im * b16r
    e_in = (S5_L - 1.0) - lax.broadcasted_iota(jnp.int32, (S5_L, 1), 0).astype(F32)
    pm = jnp.exp(e_in * (lr * dt))
    pr = pm * jnp.cos(e_in * (li * dt))
    pi = pm * jnp.sin(e_in * (li * dt))
    own = (lax.broadcasted_iota(jnp.int32, (LANES, S5_ST), 0) // SSM_GROUP
           == lax.broadcasted_iota(jnp.int32, (LANES, S5_ST), 1) // SSM_STATE)
    for s in range(S5_L):
        prs, pis = pr[s:s + 1, :], pi[s:s + 1, :]
        wre = jnp.concatenate([bbar_re * prs - bbar_im * pis] * S5_GL, axis=0)
        wim = jnp.concatenate([bbar_re * pis + bbar_im * prs] * S5_GL, axis=0)
        wre = jnp.where(own, wre, 0.0)
        wim = jnp.where(own, wim, 0.0)
        rows = slice(s * LANES, (s + 1) * LANES)
        winr_scr[rows, :] = wre
        wini_scr[rows, :] = wim
        wsin_ref[rows, :S5_ST] = wre.astype(BF16)
        wsin_ref[rows, S5_ST:] = wim.astype(BF16)
    a16_ref[:, :S5_ST] = jnp.exp(S5_L * lr * dt) * jnp.cos(S5_L * li * dt)
    a16_ref[:, S5_ST:] = jnp.exp(S5_L * lr * dt) * jnp.sin(S5_L * li * dt)

    lrc, lic = lr_c[...], li_c[...]
    dtc = jnp.exp(ls_c[...])
    e_out = lax.broadcasted_iota(jnp.int32, (1, S5_L), 1).astype(F32) + 1.0
    pm2 = jnp.exp(e_out * (lrc * dtc))
    ar2 = pm2 * jnp.cos(e_out * (lic * dtc))
    ai2 = pm2 * jnp.sin(e_out * (lic * dtc))
    own2 = (lax.broadcasted_iota(jnp.int32, (S5_ST, LANES), 0) // SSM_STATE
            == lax.broadcasted_iota(jnp.int32, (S5_ST, LANES), 1) // SSM_GROUP)
    cr = jnp.where(own2, cxr_ref[...], 0.0)
    ci = jnp.where(own2, cxi_ref[...], 0.0)
    for t in range(S5_L):
        art, ait = ar2[:, t:t + 1], ai2[:, t:t + 1]
        cols = slice(t * LANES, (t + 1) * LANES)
        wout_ref[:S5_ST, cols] = (cr * art - ci * ait).astype(BF16)
        wout_ref[S5_ST:, cols] = (-(cr * ait + ci * art)).astype(BF16)
    z = (jnp.dot(winr_scr[...], cr, precision=lax.Precision.HIGHEST, preferred_element_type=F32)
         - jnp.dot(wini_scr[...], ci, precision=lax.Precision.HIGHEST, preferred_element_type=F32))
    z_ref[...] = z.astype(BF16)


def _s5_prep(lam_re, lam_im, log_step, b_re, b_im, c_re, c_im):
    n = DEPTH * S5_NB
    blk = lambda a: a.reshape((n, S5_GL) + a.shape[2:])
    ls = jnp.broadcast_to(log_step[..., None], (DEPTH, SSM_GROUPS, SSM_STATE))
    rows = [blk(a).reshape(n, 1, S5_ST) for a in (lam_re, lam_im, ls)]
    cols = [blk(a).reshape(n, S5_ST, 1) for a in (lam_re, lam_im, ls)]
    b16 = [jnp.transpose(blk(a), (0, 3, 1, 2)).reshape(n, SSM_GROUP, S5_ST) for a in (b_re, b_im)]
    cx = [jnp.tile(jnp.swapaxes(blk(a), 2, 3).reshape(n, S5_ST, SSM_GROUP), (1, 1, S5_GL))
          for a in (c_re, c_im)]
    spec = lambda *tail: pl.BlockSpec((None,) + tail, lambda i: (i,) + (0,) * len(tail))
    outs = [((S5_IN, 2 * S5_ST), BF16), ((2 * S5_ST, S5_IN), BF16), ((S5_IN, LANES), BF16),
            ((1, 2 * S5_ST), F32)]
    return pl.pallas_call(
        _s5_prep_kernel,
        grid=(n,),
        in_specs=([spec(1, S5_ST)] * 3 + [spec(S5_ST, 1)] * 3 + [spec(SSM_GROUP, S5_ST)] * 2
                  + [spec(S5_ST, LANES)] * 2),
        out_specs=[spec(*s) for s, _ in outs],
        out_shape=[jax.ShapeDtypeStruct((n,) + s, t) for s, t in outs],
        scratch_shapes=[pltpu.VMEM((S5_IN, S5_ST), F32), pltpu.VMEM((S5_IN, S5_ST), F32)],
        compiler_params=pltpu.CompilerParams(
            dimension_semantics=("arbitrary",), vmem_limit_bytes=VMEM_LIMIT),
        name="s5_prep",
    )(*rows, *cols, *b16, *cx)


S5_CT = 32
S5_TB = 2 * LANES


def _s5_kernel(u_ref, wsin_ref, z_ref, wout_ref, a16_ref, dsk_ref, y_ref,
               sin_scr, xp_scr, carry_scr, wintra_ref):
    @pl.when(pl.program_id(1) == 0)
    def _():
        carry_scr[...] = jnp.zeros_like(carry_scr)
        for t in range(S5_L):
            keep = (t + 1) * LANES
            cols = slice(t * LANES, (t + 1) * LANES)
            wintra_ref[:keep, cols] = z_ref[S5_IN - keep:, :]
            if keep < S5_IN:
                wintra_ref[keep:, cols] = jnp.zeros((S5_IN - keep, LANES), BF16)

    sin_scr[...] = _dot(u_ref[...], wsin_ref[...])
    ar = a16_ref[:, :S5_ST]
    ai = a16_ref[:, S5_ST:]

    def step(c, carry):
        xr, xi = carry
        r0 = pl.multiple_of(c * BATCH, BATCH)
        xp_scr[pl.ds(r0, BATCH), :S5_ST] = xr
        xp_scr[pl.ds(r0, BATCH), S5_ST:] = xi
        sr = sin_scr[pl.ds(r0, BATCH), :S5_ST]
        si = sin_scr[pl.ds(r0, BATCH), S5_ST:]
        return ar * xr - ai * xi + sr, ar * xi + ai * xr + si

    xr, xi = lax.fori_loop(0, S5_CT, step, (carry_scr[:, :S5_ST], carry_scr[:, S5_ST:]), unroll=4)
    carry_scr[:, :S5_ST] = xr
    carry_scr[:, S5_ST:] = xi

    xp = xp_scr[...].astype(BF16)
    for j in range(S5_IN // S5_TB):
        cols = slice(j * S5_TB, (j + 1) * S5_TB)
        kend = (j + 1) * S5_TB
        y = (_dot(xp, wout_ref[:, cols]) + _dot(u_ref[:, :kend], wintra_ref[:kend, cols])
             + dsk_ref[:, cols] * u_ref[:, cols].astype(F32))
        y_ref[:, cols] = jax.nn.gelu(y).astype(BF16)


def _s5(u, layer, wsin, wout, z, a16, dsk):
    rows = S5_CT * BATCH
    wspec = lambda a: pl.BlockSpec((None,) + a.shape[1:], lambda g, c: (layer * S5_NB + g, 0, 0))
    xspec = pl.BlockSpec((None, rows, S5_IN), lambda g, c: (g, c, 0))
    u2 = u.reshape(S5_NB, S5_CHUNKS * BATCH, S5_IN)
    y = pl.pallas_call(
        _s5_kernel,
        grid=(S5_NB, S5_CHUNKS // S5_CT),
        in_specs=[xspec, wspec(wsin), wspec(z), wspec(wout), wspec(a16), wspec(dsk)],
        out_specs=xspec,
        out_shape=jax.ShapeDtypeStruct(u2.shape, BF16),
        scratch_shapes=[pltpu.VMEM((rows, 2 * S5_ST), F32), pltpu.VMEM((rows, 2 * S5_ST), F32),
                        pltpu.VMEM((BATCH, 2 * S5_ST), F32), pltpu.VMEM((S5_IN, S5_IN), BF16)],
        compiler_params=pltpu.CompilerParams(
            dimension_semantics=("arbitrary", "arbitrary"), vmem_limit_bytes=VMEM_LIMIT),
        name="s5",
    )(u2, wsin, z, wout, a16, dsk)
    return y.reshape(u.shape)


def _mix_out_kernel(x_ref, o_ref, yg_ref, ga_ref, gs_ref, woa_ref, wglu_ref, wout_ref, gpost_ref,
                    xo_ref):
    y_att = _dot(o_ref[...], woa_ref[...])
    yg = jnp.concatenate([yg_ref[gb].reshape(-1, LANES) for gb in range(S5_NB)], axis=-1)
    y_ssm =_dot(yg, wglu_ref[:, :D_MODEL]) * jax.nn.sigmoid(_dot(yg, wglu_ref[:, D_MODEL:]))
    merged = ga_ref[...].astype(F32) * y_att + gs_ref[...].astype(F32) * y_ssm
    m = _dot(merged.astype(BF16), wout_ref[...])
    xo_ref[...] = x_ref[...] + _rms(m, gpost_ref[...])


def _mix_out(x, o, yg, ga, gs, woa, wglu, wout, gpost, tm=512):
    row = lambda d: pl.BlockSpec((None, tm, d), lambda b, i: (b, i, 0))
    full = lambda a: pl.BlockSpec(a.shape, lambda b, i: (0,) * a.ndim)
    yg_spec = pl.BlockSpec((S5_NB, tm // S5_L, None, S5_L, LANES), lambda b, i: (0, i, b, 0, 0))
    return pl.pallas_call(
        _mix_out_kernel,
        grid=(BATCH, SEQ // tm),
        in_specs=[row(D_MODEL), row(N_HEADS * V_HEAD), yg_spec, row(D_MODEL), row(D_MODEL),
                  full(woa), full(wglu), full(wout), full(gpost)],
        out_specs=row(D_MODEL),
        out_shape=jax.ShapeDtypeStruct((BATCH, SEQ, D_MODEL), F32),
        compiler_params=pltpu.CompilerParams(
            dimension_semantics=("arbitrary", "arbitrary"), vmem_limit_bytes=VMEM_LIMIT),
        name="mix_out",
    )(x, o, yg, ga, gs, woa, wglu, wout, gpost)


FFN_HALO = 8


def _ffn_kernel(x_ref, gpre_ref, wup_ref, cw_ref, cb_ref, wdn_ref, gpost_ref, xo_ref, gate_scr):
    tm = x_ref.shape[0]
    i = pl.program_id(1)

    @pl.when(i == 0)
    def _():
        gate_scr[0:FFN_HALO, :] = jnp.zeros((FFN_HALO, D_FF), F32)

    @pl.when(i > 0)
    def _():
        gate_scr[0:FFN_HALO, :] = gate_scr[tm:tm + FFN_HALO, :]

    x = x_ref[...]
    h = _rms(x, gpre_ref[...]).astype(BF16)
    gate = _dot(h, wup_ref[:, :D_FF])
    val = _dot(h, wup_ref[:, D_FF:])
    gate_scr[FFN_HALO:, :] = gate
    conv = (cb_ref[...]
            + cw_ref[0:1, :] * gate_scr[pl.ds(FFN_HALO - 2, tm), :]
            + cw_ref[1:2, :] * gate_scr[pl.ds(FFN_HALO - 1, tm), :]
            + cw_ref[2:3, :] * gate)
    act = (jax.nn.gelu(conv) * val).astype(BF16)
    y = _dot(act, wdn_ref[...])
    xo_ref[...] = x + _rms(y, gpost_ref[...])


def _ffn(x, gpre, wup, cw, cb, wdn, gpost, tm=512):
    row = pl.BlockSpec((None, tm, D_MODEL), lambda b, i: (b, i, 0))
    full = lambda a: pl.BlockSpec(a.shape, lambda b, i: (0,) * a.ndim)
    return pl.pallas_call(
        _ffn_kernel,
        grid=(BATCH, SEQ // tm),
        in_specs=[row, full(gpre), full(wup), full(cw), full(cb), full(wdn), full(gpost)],
        out_specs=row,
        out_shape=jax.ShapeDtypeStruct((BATCH, SEQ, D_MODEL), F32),
        scratch_shapes=[pltpu.VMEM((tm + FFN_HALO, D_FF), F32)],
        compiler_params=pltpu.CompilerParams(
            dimension_semantics=("arbitrary", "arbitrary"), vmem_limit_bytes=VMEM_LIMIT),
        name="ffn",
    )(x, gpre, wup, cw, cb, wdn, gpost)


def _pad_cols(w, width):
    return jnp.pad(w, ((0, 0), (0, width - w.shape[1])))


def _layout_w_in(w):
    o1, o2, o3, o4 = Q_LORA, Q_LORA + KV_LORA, Q_LORA + KV_LORA + QK_ROPE, Q_LORA + KV_LORA + QK_ROPE + SSM_WIDTH
    return jnp.concatenate([w[:, :o1], _pad_cols(w[:, o2:o3], LANES), w[:, o1:o2], w[:, o3:o4],
                            w[:, o4:]], axis=1).astype(BF16)


def _layout_w_uq(w):
    w3 = w.reshape(Q_LORA, N_HEADS, QK_NOPE + QK_ROPE)
    pad = jnp.zeros((Q_LORA, N_HEADS, HEAD_PAD - QK_NOPE - QK_ROPE), w.dtype)
    return jnp.concatenate([w3, pad], axis=-1).reshape(Q_LORA, N_HEADS * HEAD_PAD).astype(BF16)


def _layout_w_ukv(w):
    w3 = w.reshape(KV_LORA, N_HEADS, QK_NOPE + V_HEAD)
    return (w3[..., :QK_NOPE].reshape(KV_LORA, N_HEADS * QK_NOPE).astype(BF16),
            w3[..., QK_NOPE:].reshape(KV_LORA, N_HEADS * V_HEAD).astype(BF16))


def _rope_tables():
    pos = jnp.arange(SEQ, dtype=F32)
    inv_freq = ROPE_THETA ** (-jnp.arange(0, QK_ROPE, 2, dtype=F32) / QK_ROPE)
    ang = pos[:, None] * inv_freq[None, :]
    cos, sin = jnp.cos(ang), jnp.sin(ang)
    scale = math.log2(math.e) / math.sqrt(QK_NOPE + QK_ROPE)
    ones = jnp.ones((SEQ, QK_NOPE), F32)
    zq = jnp.zeros((SEQ, HEAD_PAD - QK_NOPE - QK_ROPE), F32)
    cq = jnp.concatenate([ones, cos, cos, zq], axis=1) * scale
    sq = jnp.concatenate([0.0 * ones, -sin, sin, zq], axis=1) * scale
    zk = jnp.zeros((SEQ, LANES - QK_ROPE), F32)
    ck = jnp.concatenate([cos, cos, zk], axis=1)
    sk = jnp.concatenate([-sin, sin, zk], axis=1)
    return cq, sq, ck, sk


def kernel(x, w_in, b_gate, g_mix_pre, g_q, w_uq, g_kv, w_ukv, w_o_att, lam_re, lam_im, log_step, b_re, b_im, c_re, c_im, d_skip, w_glu, w_out, g_mix_post, g_ffn_pre, w_up, conv_w, conv_b, w_down, g_ffn_post):
    cq, sq, ck, sk = _rope_tables()
    wsin, wout_s5, z_s5, a16 = _s5_prep(lam_re, lam_im, log_step, b_re, b_im, c_re, c_im)
    dsk = jnp.tile(d_skip.reshape(DEPTH * S5_NB, 1, LANES), (1, 1, S5_L))
    cw = jnp.pad(conv_w, ((0, 0), (0, FFN_HALO - CONV_W), (0, 0)))
    row = lambda a: a[None, :]
    for l in range(DEPTH):
        wkn, wv = _layout_w_ukv(w_ukv[l])
        q, k, v, u, ga, gs = _mix_in(
            x, row(g_mix_pre[l]), _layout_w_in(w_in[l]), row(b_gate[l, :D_MODEL]),
            row(b_gate[l, D_MODEL:]), row(g_q[l]), _layout_w_uq(w_uq[l]), row(g_kv[l]), wkn, wv,
            cq, sq, ck, sk)
        o = _attn(q, k, v)
        yg = _s5(u, l, wsin, wout_s5, z_s5, a16, dsk)
        x = _mix_out(x, o, yg, ga, gs, w_o_att[l].astype(BF16), w_glu[l].astype(BF16),
                     w_out[l].astype(BF16), row(g_mix_post[l]))
        x = _ffn(x, row(g_ffn_pre[l]), w_up[l].astype(BF16), cw[l], row(conv_b[l]),
                 w_down[l].astype(BF16), row(g_ffn_post[l]))
    return x
```

```python
import math

import jax
import jax.numpy as jnp
from jax import lax
from jax.experimental import pallas as pl
from jax.experimental.pallas import tpu as pltpu

D_MODEL = 1024
BATCH = 16
SEQ = 2048
DEPTH = 4
CHUNK = 64
N_HEADS = 8
QK_NOPE = 64
QK_ROPE = 32
V_HEAD = 64
Q_LORA = 384
KV_LORA = 256
ROPE_THETA = 10000.0
SSM_WIDTH = 512
SSM_GROUP = 16
SSM_GROUPS = SSM_WIDTH // SSM_GROUP
SSM_STATE = 64
D_FF = 2816
CONV_W = 3
EPS = 1e-6

LANES = 128
HEAD_PAD = LANES
S5_L = 16
S5_CHUNKS = SEQ // S5_L
S5_GL = LANES // SSM_GROUP
S5_NB = SSM_GROUPS // S5_GL
S5_IN = S5_L * LANES
S5_ST = S5_GL * SSM_STATE
VMEM_LIMIT = 56 * 1024 * 1024

F32 = jnp.float32
BF16 = jnp.bfloat16


def _rms(x, g):
    var = jnp.mean(x * x, axis=-1, keepdims=True)
    return x * lax.rsqrt(var + EPS) * g


def _dot(a, b):
    return jnp.dot(a, b, preferred_element_type=F32)


def _dot_nt(a, b):
    return lax.dot_general(a, b, (((1,), (1,)), ((), ())), preferred_element_type=F32)


C_Q0, C_KPE0, C_KV0, C_U0, C_GA0, C_GS0, C_END = 0, 384, 512, 768, 1280, 2304, 3328
ROPE_HALF = QK_ROPE // 2


def _swap_halves(x, lo):
    lane = lax.broadcasted_iota(jnp.int32, x.shape, 1)
    first = (lane >= lo) & (lane < lo + ROPE_HALF)
    return jnp.where(first, pltpu.roll(x, LANES - ROPE_HALF, axis=1), pltpu.roll(x, ROPE_HALF, axis=1))


def _mix_in_kernel(x_ref, gpre_ref, win_ref, bg_ref, gq_ref, wuq_ref,
                   gkv_ref, wkn_ref, wv_ref, cq_ref, sq_ref, ck_ref, sk_ref,
                   q_ref, k_ref, v_ref, u_ref, ga_ref, gs_ref):
    h = _rms(x_ref[...], gpre_ref[...]).astype(BF16)

    z_q = _dot(h, win_ref[:, C_Q0:C_KV0])
    cqn = _rms(z_q[:, :Q_LORA], gq_ref[...]).astype(BF16)
    q = _dot(cqn, wuq_ref[...])
    cq, sq = cq_ref[...], sq_ref[...]
    for hd in range(N_HEADS):
        sl = slice(hd * HEAD_PAD, (hd + 1) * HEAD_PAD)
        qh = q[:, sl]
        q_ref[:, sl] = (qh * cq + _swap_halves(qh, QK_NOPE) * sq).astype(BF16)

    kpe = z_q[:, C_KPE0:C_KV0]
    kpe = pltpu.roll(kpe * ck_ref[...] + _swap_halves(kpe, 0) * sk_ref[...], QK_NOPE, axis=1)

    c_kv = _dot(h, win_ref[:, C_KV0:C_U0])
    ckvn = _rms(c_kv, gkv_ref[...]).astype(BF16)
    kn = _dot(ckvn, wkn_ref[...])
    lane = lax.broadcasted_iota(jnp.int32, kpe.shape, 1)
    for hp in range(N_HEADS // 2):
        pair = kn[:, hp * LANES:(hp + 1) * LANES]
        for hh, nope in enumerate((pair, pltpu.roll(pair, QK_NOPE, axis=1))):
            sl = slice((2 * hp + hh) * HEAD_PAD, (2 * hp + hh + 1) * HEAD_PAD)
            k_ref[:, sl] = jnp.where(lane < QK_NOPE, nope, kpe).astype(BF16)
    v_ref[...] = _dot(ckvn, wv_ref[...]).astype(BF16)

    u = _dot(h, win_ref[:, C_U0:C_GA0])
    for gb in range(S5_NB):
        ub = u[:, gb * LANES:(gb + 1) * LANES].reshape(-1, S5_L, LANES)
        u_ref[gb] = ub.astype(BF16)
    ga_ref[...] = jax.nn.sigmoid(
        _dot(h, win_ref[:, C_GA0:C_GS0]) + bg_ref[:, :D_MODEL]).astype(BF16)
    gs_ref[...] = jax.nn.sigmoid(
        _dot(h, win_ref[:, C_GS0:C_END]) + bg_ref[:, D_MODEL:]).astype(BF16)


def _layer_spec(a, layer):
    return pl.BlockSpec((None,) + a.shape[1:], lambda *_: (layer,) + (0,) * (a.ndim - 1))


def _mix_in(x, layer, gpre, win, bg, gq, wuq, gkv, wkn, wv, cq, sq, ck, sk, tm=512):
    nt = SEQ // tm
    row = lambda d: pl.BlockSpec((None, tm, d), lambda i, b: (b, i, 0))
    full = lambda a: _layer_spec(a, layer)
    tab = pl.BlockSpec((tm, LANES), lambda i, b: (i, 0))
    hq = N_HEADS * HEAD_PAD
    outs = [(hq, BF16), (hq, BF16), (N_HEADS * V_HEAD, BF16), None, (D_MODEL, BF16), (D_MODEL, BF16)]
    u_spec = pl.BlockSpec((S5_NB, tm // S5_L, None, S5_L, LANES), lambda i, b: (0, i, b, 0, 0))
    u_shape = jax.ShapeDtypeStruct((S5_NB, S5_CHUNKS, BATCH, S5_L, LANES), BF16)
    return pl.pallas_call(
        _mix_in_kernel,
        grid=(nt, BATCH),
        in_specs=[row(D_MODEL), full(gpre), full(win), full(bg), full(gq), full(wuq),
                  full(gkv), full(wkn), full(wv), tab, tab, tab, tab],
        out_specs=[u_spec if o is None else row(o[0]) for o in outs],
        out_shape=[u_shape if o is None else jax.ShapeDtypeStruct((BATCH, SEQ, o[0]), o[1])
                   for o in outs],
        compiler_params=pltpu.CompilerParams(
            dimension_semantics=("arbitrary", "arbitrary"), vmem_limit_bytes=VMEM_LIMIT),
        name="mix_in",
    )(x, gpre, win, bg, gq, wuq, gkv, wkn, wv, cq, sq, ck, sk)


ATT_TQ = 256


def _attn_kernel(q_ref, k_ref, v_ref, o_ref):
    tq = ATT_TQ
    row_chunk = lax.broadcasted_iota(jnp.int32, (tq, tq), 0) // CHUNK
    col_chunk = lax.broadcasted_iota(jnp.int32, (tq, tq), 1) // CHUNK
    mask = row_chunk >= col_chunk
    lane = lax.broadcasted_iota(jnp.int32, (tq, 2 * V_HEAD), 1)
    for i in range(SEQ // tq):
        q0 = i * tq
        outs = []
        for hh in range(2):
            hs = slice(hh * HEAD_PAD, (hh + 1) * HEAD_PAD)
            qh = q_ref[q0:q0 + tq, hs]
            s_d = jnp.where(mask, _dot_nt(qh, k_ref[q0:q0 + tq, hs]), -jnp.inf)
            m = jnp.max(s_d, axis=-1, keepdims=True)
            if i > 0:
                s_p = _dot_nt(qh, k_ref[0:q0, hs])
                m = jnp.maximum(m, jnp.max(s_p, axis=-1, keepdims=True))
            p_d = jnp.exp2(s_d - m)
            l = jnp.sum(p_d, axis=-1, keepdims=True)
            acc = _dot(p_d.astype(BF16), v_ref[q0:q0 + tq, :])
            if i > 0:
                p_p = jnp.exp2(s_p - m)
                l = l + jnp.sum(p_p, axis=-1, keepdims=True)
                acc = acc + _dot(p_p.astype(BF16), v_ref[0:q0, :])
            outs.append(acc / l)
        o_ref[q0:q0 + tq, :] = jnp.where(lane < V_HEAD, outs[0], outs[1]).astype(BF16)


def _attn(q, k, v):
    blk = lambda d: pl.BlockSpec((None, SEQ, d), lambda b, p: (b, 0, p))
    return pl.pallas_call(
        _attn_kernel,
        grid=(BATCH, N_HEADS // 2),
        in_specs=[blk(2 * HEAD_PAD), blk(2 * HEAD_PAD), blk(2 * V_HEAD)],
        out_specs=blk(2 * V_HEAD),
        out_shape=jax.ShapeDtypeStruct((BATCH, SEQ, N_HEADS * V_HEAD), BF16),
        compiler_params=pltpu.CompilerParams(
            dimension_semantics=("arbitrary", "arbitrary"), vmem_limit_bytes=VMEM_LIMIT),
        name="attn",
    )(q, k, v)


def _s5_prep_kernel(lr_r, li_r, ls_r, lr_c, li_c, ls_c, b16r_ref, b16i_ref, cxr_ref, cxi_ref,
                    wsin_ref, wout_ref, z_ref, a16_ref):
    lr, li = lr_r[...], li_r[...]
    dt = jnp.exp(ls_r[...])
    mag = jnp.exp(lr * dt)
    abar_re = mag * jnp.cos(li * dt)
    abar_im = mag * jnp.sin(li * dt)
    nr, ni = abar_re - 1.0, abar_im
    den = lr * lr + li * li
    f_re = (nr * lr + ni * li) / den
    f_im = (ni * lr - nr * li) / den
    b16r, b16i = b16r_ref[...], b16i_ref[...]
    bbar_re = f_re * b16r - f_im * b16i
    bbar_im = f_re * b16i + f_im * b16r
    e_in = (S5_L - 1.0) - lax.broadcasted_iota(jnp.int32, (S5_L, 1), 0).astype(F32)
    pm = jnp.exp(e_in * (lr * dt))
    pr = pm * jnp.cos(e_in * (li * dt))
    pi = pm * jnp.sin(e_in * (li * dt))
    own = (lax.broadcasted_iota(jnp.int32, (LANES, S5_ST), 0) // SSM_GROUP
           == lax.broadcasted_iota(jnp.int32, (LANES, S5_ST), 1) // SSM_STATE)
    for s in range(S5_L):
        prs, pis = pr[s:s + 1, :], pi[s:s + 1, :]
        wre = jnp.concatenate([bbar_re * prs - bbar_im * pis] * S5_GL, axis=0)
        wim = jnp.concatenate([bbar_re * pis + bbar_im * prs] * S5_GL, axis=0)
        wre = jnp.where(own, wre, 0.0)
        wim = jnp.where(own, wim, 0.0)
        rows = slice(s * LANES, (s + 1) * LANES)
        wsin_ref[rows, :S5_ST] = wre.astype(BF16)
        wsin_ref[rows, S5_ST:] = wim.astype(BF16)
    a16_ref[:, :S5_ST] = jnp.exp(S5_L * lr * dt) * jnp.cos(S5_L * li * dt)
    a16_ref[:, S5_ST:] = jnp.exp(S5_L * lr * dt) * jnp.sin(S5_L * li * dt)

    lrc, lic = lr_c[...], li_c[...]
    dtc = jnp.exp(ls_c[...])
    e_out = lax.broadcasted_iota(jnp.int32, (1, S5_L), 1).astype(F32) + 1.0
    pm2 = jnp.exp(e_out * (lrc * dtc))
    ar2 = pm2 * jnp.cos(e_out * (lic * dtc))
    ai2 = pm2 * jnp.sin(e_out * (lic * dtc))
    own2 = (lax.broadcasted_iota(jnp.int32, (S5_ST, LANES), 0) // SSM_STATE
            == lax.broadcasted_iota(jnp.int32, (S5_ST, LANES), 1) // SSM_GROUP)
    cr = jnp.where(own2, cxr_ref[...], 0.0)
    ci = jnp.where(own2, cxi_ref[...], 0.0)
    for t in range(S5_L):
        art, ait = ar2[:, t:t + 1], ai2[:, t:t + 1]
        cols = slice(t * LANES, (t + 1) * LANES)
        wout_ref[:S5_ST, cols] = (cr * art - ci * ait).astype(BF16)
        wout_ref[S5_ST:, cols] = (-(cr * ait + ci * art)).astype(BF16)
    z = _dot(wsin_ref[...], jnp.concatenate([cr, -ci], axis=0).astype(BF16))
    z_ref[...] = z.astype(BF16)


def _s5_prep(lam_re, lam_im, log_step, b_re, b_im, c_re, c_im):
    n = DEPTH * S5_NB
    blk = lambda a: a.reshape((n, S5_GL) + a.shape[2:])
    ls = jnp.broadcast_to(log_step[..., None], (DEPTH, SSM_GROUPS, SSM_STATE))
    rows = [blk(a).reshape(n, 1, S5_ST) for a in (lam_re, lam_im, ls)]
    cols = [blk(a).reshape(n, S5_ST, 1) for a in (lam_re, lam_im, ls)]
    b16 = [jnp.transpose(blk(a), (0, 3, 1, 2)).reshape(n, SSM_GROUP, S5_ST) for a in (b_re, b_im)]
    cx = [jnp.tile(jnp.swapaxes(blk(a), 2, 3).reshape(n, S5_ST, SSM_GROUP), (1, 1, S5_GL))
          for a in (c_re, c_im)]
    spec = lambda *tail: pl.BlockSpec((None,) + tail, lambda i: (i,) + (0,) * len(tail))
    outs = [((S5_IN, 2 * S5_ST), BF16), ((2 * S5_ST, S5_IN), BF16), ((S5_IN, LANES), BF16),
            ((1, 2 * S5_ST), F32)]
    return pl.pallas_call(
        _s5_prep_kernel,
        grid=(n,),
        in_specs=([spec(1, S5_ST)] * 3 + [spec(S5_ST, 1)] * 3 + [spec(SSM_GROUP, S5_ST)] * 2
                  + [spec(S5_ST, LANES)] * 2),
        out_specs=[spec(*s) for s, _ in outs],
        out_shape=[jax.ShapeDtypeStruct((n,) + s, t) for s, t in outs],
        compiler_params=pltpu.CompilerParams(
            dimension_semantics=("arbitrary",), vmem_limit_bytes=VMEM_LIMIT),
        name="s5_prep",
    )(*rows, *cols, *b16, *cx)


S5_CT = 64
S5_TB = 2 * LANES


def _s5_kernel(u_ref, wsin_ref, z_ref, wout_ref, a16_ref, dsk_ref, y_ref,
               sin_scr, xp_scr, carry_scr, wintra_ref):
    @pl.when(pl.program_id(1) == 0)
    def _():
        carry_scr[...] = jnp.zeros_like(carry_scr)
        for t in range(S5_L):
            keep = (t + 1) * LANES
            cols = slice(t * LANES, (t + 1) * LANES)
            wintra_ref[:keep, cols] = z_ref[S5_IN - keep:, :]
            if keep < S5_IN:
                wintra_ref[keep:, cols] = jnp.zeros((S5_IN - keep, LANES), BF16)

    sin_scr[...] = _dot(u_ref[...], wsin_ref[...])
    ar = a16_ref[:, :S5_ST]
    ai = a16_ref[:, S5_ST:]

    def step(c, carry):
        xr, xi = carry
        r0 = pl.multiple_of(c * BATCH, BATCH)
        xp_scr[pl.ds(r0, BATCH), :S5_ST] = xr
        xp_scr[pl.ds(r0, BATCH), S5_ST:] = xi
        sr = sin_scr[pl.ds(r0, BATCH), :S5_ST]
        si = sin_scr[pl.ds(r0, BATCH), S5_ST:]
        return ar * xr - ai * xi + sr, ar * xi + ai * xr + si

    xr, xi = lax.fori_loop(0, S5_CT, step, (carry_scr[:, :S5_ST], carry_scr[:, S5_ST:]), unroll=4)
    carry_scr[:, :S5_ST] = xr
    carry_scr[:, S5_ST:] = xi

    xp = xp_scr[...].astype(BF16)
    for j in range(S5_IN // S5_TB):
        cols = slice(j * S5_TB, (j + 1) * S5_TB)
        kend = (j + 1) * S5_TB
        y = (_dot(xp, wout_ref[:, cols]) + _dot(u_ref[:, :kend], wintra_ref[:kend, cols])
             + dsk_ref[:, cols] * u_ref[:, cols].astype(F32))
        y_ref[:, cols] = jax.nn.gelu(y).astype(BF16)


def _s5(u, layer, wsin, wout, z, a16, dsk):
    rows = S5_CT * BATCH
    wspec = lambda a: pl.BlockSpec((None,) + a.shape[1:], lambda g, c: (layer * S5_NB + g, 0, 0))
    xspec = pl.BlockSpec((None, rows, S5_IN), lambda g, c: (g, c, 0))
    u2 = u.reshape(S5_NB, S5_CHUNKS * BATCH, S5_IN)
    y = pl.pallas_call(
        _s5_kernel,
        grid=(S5_NB, S5_CHUNKS // S5_CT),
        in_specs=[xspec, wspec(wsin), wspec(z), wspec(wout), wspec(a16), wspec(dsk)],
        out_specs=xspec,
        out_shape=jax.ShapeDtypeStruct(u2.shape, BF16),
        scratch_shapes=[pltpu.VMEM((rows, 2 * S5_ST), F32), pltpu.VMEM((rows, 2 * S5_ST), F32),
                        pltpu.VMEM((BATCH, 2 * S5_ST), F32), pltpu.VMEM((S5_IN, S5_IN), BF16)],
        compiler_params=pltpu.CompilerParams(
            dimension_semantics=("arbitrary", "arbitrary"), vmem_limit_bytes=VMEM_LIMIT),
        name="s5",
    )(u2, wsin, z, wout, a16, dsk)
    return y.reshape(u.shape)


def _mix_out_kernel(x_ref, o_ref, yg_ref, ga_ref, gs_ref, woa_ref, wglu_ref, wout_ref, gpost_ref,
                    xo_ref):
    y_att = _dot(o_ref[...], woa_ref[...])
    yg = jnp.concatenate([yg_ref[gb].reshape(-1, LANES) for gb in range(S5_NB)], axis=-1)
    y_ssm =_dot(yg, wglu_ref[:, :D_MODEL]) * jax.nn.sigmoid(_dot(yg, wglu_ref[:, D_MODEL:]))
    merged = ga_ref[...].astype(F32) * y_att + gs_ref[...].astype(F32) * y_ssm
    m = _dot(merged.astype(BF16), wout_ref[...])
    xo_ref[...] = x_ref[...] + _rms(m, gpost_ref[...])


def _mix_out(x, layer, o, yg, ga, gs, woa, wglu, wout, gpost, tm=512):
    row = lambda d: pl.BlockSpec((None, tm, d), lambda b, i: (b, i, 0))
    full = lambda a: _layer_spec(a, layer)
    yg_spec = pl.BlockSpec((S5_NB, tm // S5_L, None, S5_L, LANES), lambda b, i: (0, i, b, 0, 0))
    return pl.pallas_call(
        _mix_out_kernel,
        grid=(BATCH, SEQ // tm),
        in_specs=[row(D_MODEL), row(N_HEADS * V_HEAD), yg_spec, row(D_MODEL), row(D_MODEL),
                  full(woa), full(wglu), full(wout), full(gpost)],
        out_specs=row(D_MODEL),
        out_shape=jax.ShapeDtypeStruct((BATCH, SEQ, D_MODEL), F32),
        compiler_params=pltpu.CompilerParams(
            dimension_semantics=("arbitrary", "arbitrary"), vmem_limit_bytes=VMEM_LIMIT),
        name="mix_out",
    )(x, o, yg, ga, gs, woa, wglu, wout, gpost)


FFN_HALO = 8


def _ffn_kernel(x_ref, gpre_ref, wup_ref, cw_ref, cb_ref, wdn_ref, gpost_ref, xo_ref, gate_scr):
    tm = x_ref.shape[0]
    i = pl.program_id(1)

    @pl.when(i == 0)
    def _():
        gate_scr[0:FFN_HALO, :] = jnp.zeros((FFN_HALO, D_FF), F32)

    @pl.when(i > 0)
    def _():
        gate_scr[0:FFN_HALO, :] = gate_scr[tm:tm + FFN_HALO, :]

    x = x_ref[...]
    h = _rms(x, gpre_ref[...]).astype(BF16)
    gate = _dot(h, wup_ref[:, :D_FF])
    val = _dot(h, wup_ref[:, D_FF:])
    gate_scr[FFN_HALO:, :] = gate
    conv = (cb_ref[...]
            + cw_ref[0:1, :] * gate_scr[pl.ds(FFN_HALO - 2, tm), :]
            + cw_ref[1:2, :] * gate_scr[pl.ds(FFN_HALO - 1, tm), :]
            + cw_ref[2:3, :] * gate)
    act = (jax.nn.gelu(conv) * val).astype(BF16)
    y = _dot(act, wdn_ref[...])
    xo_ref[...] = x + _rms(y, gpost_ref[...])


def _ffn(x, layer, gpre, wup, cw, cb, wdn, gpost, tm=512):
    row = pl.BlockSpec((None, tm, D_MODEL), lambda b, i: (b, i, 0))
    full = lambda a: _layer_spec(a, layer)
    return pl.pallas_call(
        _ffn_kernel,
        grid=(BATCH, SEQ // tm),
        in_specs=[row, full(gpre), full(wup), full(cw), full(cb), full(wdn), full(gpost)],
        out_specs=row,
        out_shape=jax.ShapeDtypeStruct((BATCH, SEQ, D_MODEL), F32),
        scratch_shapes=[pltpu.VMEM((tm + FFN_HALO, D_FF), F32)],
        compiler_params=pltpu.CompilerParams(
            dimension_semantics=("arbitrary", "arbitrary"), vmem_limit_bytes=VMEM_LIMIT),
        name="ffn",
    )(x, gpre, wup, cw, cb, wdn, gpost)


W_IN_ROWS = 256


def _w_in_layout_kernel(w_ref, o_ref):
    o1 = Q_LORA
    o2 = o1 + KV_LORA
    o3 = o2 + QK_ROPE
    o4 = o3 + SSM_WIDTH
    o_ref[:, C_Q0:C_KPE0] = w_ref[:, :o1].astype(BF16)
    o_ref[:, C_KPE0:C_KV0] = jnp.zeros((W_IN_ROWS, C_KV0 - C_KPE0), BF16)
    o_ref[:, C_KPE0:C_KPE0 + QK_ROPE] = w_ref[:, o2:o3].astype(BF16)
    o_ref[:, C_KV0:C_U0] = w_ref[:, o1:o2].astype(BF16)
    o_ref[:, C_U0:C_GA0] = w_ref[:, o3:o4].astype(BF16)
    o_ref[:, C_GA0:C_END] = w_ref[:, o4:].astype(BF16)


def _w_in_layout(w_in):
    cols = w_in.shape[-1]
    return pl.pallas_call(
        _w_in_layout_kernel,
        grid=(DEPTH, D_MODEL // W_IN_ROWS),
        in_specs=[pl.BlockSpec((None, W_IN_ROWS, cols), lambda l, r: (l, r, 0))],
        out_specs=pl.BlockSpec((None, W_IN_ROWS, C_END), lambda l, r: (l, r, 0)),
        out_shape=jax.ShapeDtypeStruct((DEPTH, D_MODEL, C_END), BF16),
        compiler_params=pltpu.CompilerParams(
            dimension_semantics=("arbitrary", "arbitrary"), vmem_limit_bytes=VMEM_LIMIT),
        name="w_in_layout",
    )(w_in)


def _layout_w_uq(w):
    w4 = w.reshape(DEPTH, Q_LORA, N_HEADS, QK_NOPE + QK_ROPE)
    pad = jnp.zeros((DEPTH, Q_LORA, N_HEADS, HEAD_PAD - QK_NOPE - QK_ROPE), w.dtype)
    return jnp.concatenate([w4, pad], axis=-1).reshape(DEPTH, Q_LORA, N_HEADS * HEAD_PAD).astype(BF16)


def _layout_w_ukv(w):
    w4 = w.reshape(DEPTH, KV_LORA, N_HEADS, QK_NOPE + V_HEAD)
    return (w4[..., :QK_NOPE].reshape(DEPTH, KV_LORA, N_HEADS * QK_NOPE).astype(BF16),
            w4[..., QK_NOPE:].reshape(DEPTH, KV_LORA, N_HEADS * V_HEAD).astype(BF16))


def _rope_tables():
    pos = jnp.arange(SEQ, dtype=F32)
    inv_freq = ROPE_THETA ** (-jnp.arange(0, QK_ROPE, 2, dtype=F32) / QK_ROPE)
    ang = pos[:, None] * inv_freq[None, :]
    cos, sin = jnp.cos(ang), jnp.sin(ang)
    scale = math.log2(math.e) / math.sqrt(QK_NOPE + QK_ROPE)
    ones = jnp.ones((SEQ, QK_NOPE), F32)
    zq = jnp.zeros((SEQ, HEAD_PAD - QK_NOPE - QK_ROPE), F32)
    cq = jnp.concatenate([ones, cos, cos, zq], axis=1) * scale
    sq = jnp.concatenate([0.0 * ones, -sin, sin, zq], axis=1) * scale
    zk = jnp.zeros((SEQ, LANES - QK_ROPE), F32)
    ck = jnp.concatenate([cos, cos, zk], axis=1)
    sk = jnp.concatenate([-sin, sin, zk], axis=1)
    return cq, sq, ck, sk


def kernel(x, w_in, b_gate, g_mix_pre, g_q, w_uq, g_kv, w_ukv, w_o_att, lam_re, lam_im, log_step, b_re, b_im, c_re, c_im, d_skip, w_glu, w_out, g_mix_post, g_ffn_pre, w_up, conv_w, conv_b, w_down, g_ffn_post):
    cq, sq, ck, sk = _rope_tables()
    wsin, wout_s5, z_s5, a16 = _s5_prep(lam_re, lam_im, log_step, b_re, b_im, c_re, c_im)
    dsk = jnp.tile(d_skip.reshape(DEPTH * S5_NB, 1, LANES), (1, 1, S5_L))
    cw = jnp.pad(conv_w, ((0, 0), (0, FFN_HALO - CONV_W), (0, 0)))
    vec = lambda a: a[:, None, :]
    bf = lambda a: a.astype(BF16)
    win = _w_in_layout(w_in)
    wuq = _layout_w_uq(w_uq)
    wkn, wv = _layout_w_ukv(w_ukv)
    woa, wglu, wout, wup, wdn = bf(w_o_att), bf(w_glu), bf(w_out), bf(w_up), bf(w_down)
    gpre, bg, gq, gkv, gpost = vec(g_mix_pre), vec(b_gate), vec(g_q), vec(g_kv), vec(g_mix_post)
    fpre, cb, fpost = vec(g_ffn_pre), vec(conv_b), vec(g_ffn_post)
    for l in range(DEPTH):
        q, k, v, u, ga, gs = _mix_in(x, l, gpre, win, bg, gq, wuq, gkv, wkn, wv, cq, sq, ck, sk)
        o = _attn(q, k, v)
        yg = _s5(u, l, wsin, wout_s5, z_s5, a16, dsk)
        x = _mix_out(x, l, o, yg, ga, gs, woa, wglu, wout, gpost)
        x = _ffn(x, l, fpre, wup, cw, cb, wdn, fpost)
    return x
```

```python
import math

import jax
import jax.numpy as jnp
from jax import lax
from jax.experimental import pallas as pl
from jax.experimental.pallas import tpu as pltpu

D_MODEL = 1024
BATCH = 16
SEQ = 2048
DEPTH = 4
CHUNK = 64
N_HEADS = 8
QK_NOPE = 64
QK_ROPE = 32
V_HEAD = 64
Q_LORA = 384
KV_LORA = 256
ROPE_THETA = 10000.0
SSM_WIDTH = 512
SSM_GROUP = 16
SSM_GROUPS = SSM_WIDTH // SSM_GROUP
SSM_STATE = 64
D_FF = 2816
CONV_W = 3
EPS = 1e-6

LANES = 128
HEAD_PAD = LANES
S5_L = 16
S5_CHUNKS = SEQ // S5_L
S5_GL = LANES // SSM_GROUP
S5_NB = SSM_GROUPS // S5_GL
S5_IN = S5_L * LANES
S5_ST = S5_GL * SSM_STATE
VMEM_LIMIT = 56 * 1024 * 1024

F32 = jnp.float32
BF16 = jnp.bfloat16


def _rms(x, g):
    var = jnp.mean(x * x, axis=-1, keepdims=True)
    return x * lax.rsqrt(var + EPS) * g


def _dot(a, b):
    return jnp.dot(a, b, preferred_element_type=F32)


def _dot_nt(a, b):
    return lax.dot_general(a, b, (((1,), (1,)), ((), ())), preferred_element_type=F32)


C_Q0, C_KPE0, C_KV0, C_U0, C_GA0, C_GS0, C_END = 0, 384, 512, 768, 1280, 2304, 3328
ROPE_HALF = QK_ROPE // 2


def _swap_halves(x, lo):
    lane = lax.broadcasted_iota(jnp.int32, x.shape, 1)
    first = (lane >= lo) & (lane < lo + ROPE_HALF)
    return jnp.where(first, pltpu.roll(x, LANES - ROPE_HALF, axis=1), pltpu.roll(x, ROPE_HALF, axis=1))


def _mix_in_kernel(x_ref, gpre_ref, win_ref, bg_ref, gq_ref, wuq_ref,
                   gkv_ref, wkn_ref, wv_ref, cq_ref, sq_ref, ck_ref, sk_ref,
                   q_ref, k_ref, v_ref, u_ref, ga_ref, gs_ref):
    h = _rms(x_ref[...], gpre_ref[...]).astype(BF16)

    z_q = _dot(h, win_ref[:, C_Q0:C_KV0])
    cqn = _rms(z_q[:, :Q_LORA], gq_ref[...]).astype(BF16)
    q = _dot(cqn, wuq_ref[...])
    cq, sq = cq_ref[...], sq_ref[...]
    for hd in range(N_HEADS):
        sl = slice(hd * HEAD_PAD, (hd + 1) * HEAD_PAD)
        qh = q[:, sl]
        q_ref[:, sl] = (qh * cq + _swap_halves(qh, QK_NOPE) * sq).astype(BF16)

    kpe = z_q[:, C_KPE0:C_KV0]
    kpe = pltpu.roll(kpe * ck_ref[...] + _swap_halves(kpe, 0) * sk_ref[...], QK_NOPE, axis=1)

    c_kv = _dot(h, win_ref[:, C_KV0:C_U0])
    ckvn = _rms(c_kv, gkv_ref[...]).astype(BF16)
    kn = _dot(ckvn, wkn_ref[...])
    lane = lax.broadcasted_iota(jnp.int32, kpe.shape, 1)
    for hp in range(N_HEADS // 2):
        pair = kn[:, hp * LANES:(hp + 1) * LANES]
        for hh, nope in enumerate((pair, pltpu.roll(pair, QK_NOPE, axis=1))):
            sl = slice((2 * hp + hh) * HEAD_PAD, (2 * hp + hh + 1) * HEAD_PAD)
            k_ref[:, sl] = jnp.where(lane < QK_NOPE, nope, kpe).astype(BF16)
    v_ref[...] = _dot(ckvn, wv_ref[...]).astype(BF16)

    u = _dot(h, win_ref[:, C_U0:C_GA0])
    for gb in range(S5_NB):
        u_ref[gb] = u[:, gb * LANES:(gb + 1) * LANES]
    ga_ref[...] = jax.nn.sigmoid(
        _dot(h, win_ref[:, C_GA0:C_GS0]) + bg_ref[:, :D_MODEL]).astype(BF16)
    gs_ref[...] = jax.nn.sigmoid(
        _dot(h, win_ref[:, C_GS0:C_END]) + bg_ref[:, D_MODEL:]).astype(BF16)


def _layer_spec(a, layer):
    return pl.BlockSpec((None,) + a.shape[1:], lambda *_: (layer,) + (0,) * (a.ndim - 1))


def _mix_in(x, layer, gpre, win, bg, gq, wuq, gkv, wkn, wv, cq, sq, ck, sk, tm=512):
    nt = SEQ // tm
    row = lambda d: pl.BlockSpec((None, tm, d), lambda i, b: (b, i, 0))
    full = lambda a: _layer_spec(a, layer)
    tab = pl.BlockSpec((tm, LANES), lambda i, b: (i, 0))
    hq = N_HEADS * HEAD_PAD
    outs = [(hq, BF16), (hq, BF16), (N_HEADS * V_HEAD, BF16), None, (D_MODEL, BF16), (D_MODEL, BF16)]
    u_spec = pl.BlockSpec((S5_NB, None, tm, LANES), lambda i, b: (0, b, i, 0))
    u_shape = jax.ShapeDtypeStruct((S5_NB, BATCH, SEQ, LANES), F32)
    return pl.pallas_call(
        _mix_in_kernel,
        grid=(nt, BATCH),
        in_specs=[row(D_MODEL), full(gpre), full(win), full(bg), full(gq), full(wuq),
                  full(gkv), full(wkn), full(wv), tab, tab, tab, tab],
        out_specs=[u_spec if o is None else row(o[0]) for o in outs],
        out_shape=[u_shape if o is None else jax.ShapeDtypeStruct((BATCH, SEQ, o[0]), o[1])
                   for o in outs],
        compiler_params=pltpu.CompilerParams(
            dimension_semantics=("arbitrary", "arbitrary"), vmem_limit_bytes=VMEM_LIMIT),
        name="mix_in",
    )(x, gpre, win, bg, gq, wuq, gkv, wkn, wv, cq, sq, ck, sk)


ATT_TQ = 256


def _attn_kernel(q_ref, k_ref, v_ref, o_ref):
    tq = ATT_TQ
    row_chunk = lax.broadcasted_iota(jnp.int32, (tq, tq), 0) // CHUNK
    col_chunk = lax.broadcasted_iota(jnp.int32, (tq, tq), 1) // CHUNK
    mask = row_chunk >= col_chunk
    lane = lax.broadcasted_iota(jnp.int32, (tq, 2 * V_HEAD), 1)
    for i in range(SEQ // tq):
        q0 = i * tq
        outs = []
        for hh in range(2):
            hs = slice(hh * HEAD_PAD, (hh + 1) * HEAD_PAD)
            qh = q_ref[q0:q0 + tq, hs]
            s_d = jnp.where(mask, _dot_nt(qh, k_ref[q0:q0 + tq, hs]), -jnp.inf)
            m = jnp.max(s_d, axis=-1, keepdims=True)
            if i > 0:
                s_p = _dot_nt(qh, k_ref[0:q0, hs])
                m = jnp.maximum(m, jnp.max(s_p, axis=-1, keepdims=True))
            p_d = jnp.exp2(s_d - m)
            l = jnp.sum(p_d, axis=-1, keepdims=True)
            acc = _dot(p_d.astype(BF16), v_ref[q0:q0 + tq, :])
            if i > 0:
                p_p = jnp.exp2(s_p - m)
                l = l + jnp.sum(p_p, axis=-1, keepdims=True)
                acc = acc + _dot(p_p.astype(BF16), v_ref[0:q0, :])
            outs.append(acc / l)
        o_ref[q0:q0 + tq, :] = jnp.where(lane < V_HEAD, outs[0], outs[1]).astype(BF16)


def _attn(q, k, v):
    blk = lambda d: pl.BlockSpec((None, SEQ, d), lambda b, p: (b, 0, p))
    return pl.pallas_call(
        _attn_kernel,
        grid=(BATCH, N_HEADS // 2),
        in_specs=[blk(2 * HEAD_PAD), blk(2 * HEAD_PAD), blk(2 * V_HEAD)],
        out_specs=blk(2 * V_HEAD),
        out_shape=jax.ShapeDtypeStruct((BATCH, SEQ, N_HEADS * V_HEAD), BF16),
        compiler_params=pltpu.CompilerParams(
            dimension_semantics=("arbitrary", "arbitrary"), vmem_limit_bytes=VMEM_LIMIT),
        name="attn",
    )(q, k, v)


def _s5_prep_kernel(lr_r, li_r, ls_r, lr_c, li_c, ls_c, b16r_ref, b16i_ref, cxr_ref, cxi_ref,
                    wsin_ref, wout_ref, z_ref, a16_ref):
    lr, li = lr_r[...], li_r[...]
    dt = jnp.exp(ls_r[...])
    mag = jnp.exp(lr * dt)
    abar_re = mag * jnp.cos(li * dt)
    abar_im = mag * jnp.sin(li * dt)
    nr, ni = abar_re - 1.0, abar_im
    den = lr * lr + li * li
    f_re = (nr * lr + ni * li) / den
    f_im = (ni * lr - nr * li) / den
    b16r, b16i = b16r_ref[...], b16i_ref[...]
    bbar_re = f_re * b16r - f_im * b16i
    bbar_im = f_re * b16i + f_im * b16r
    e_in = (S5_L - 1.0) - lax.broadcasted_iota(jnp.int32, (S5_L, 1), 0).astype(F32)
    pm = jnp.exp(e_in * (lr * dt))
    pr = pm * jnp.cos(e_in * (li * dt))
    pi = pm * jnp.sin(e_in * (li * dt))
    own = (lax.broadcasted_iota(jnp.int32, (LANES, S5_ST), 0) // SSM_GROUP
           == lax.broadcasted_iota(jnp.int32, (LANES, S5_ST), 1) // SSM_STATE)
    for s in range(S5_L):
        prs, pis = pr[s:s + 1, :], pi[s:s + 1, :]
        wre = jnp.concatenate([bbar_re * prs - bbar_im * pis] * S5_GL, axis=0)
        wim = jnp.concatenate([bbar_re * pis + bbar_im * prs] * S5_GL, axis=0)
        wre = jnp.where(own, wre, 0.0)
        wim = jnp.where(own, wim, 0.0)
        rows = slice(s * LANES, (s + 1) * LANES)
        wsin_ref[rows, :S5_ST] = wre.astype(BF16)
        wsin_ref[rows, S5_ST:] = wim.astype(BF16)
    a16_ref[:, :S5_ST] = jnp.exp(S5_L * lr * dt) * jnp.cos(S5_L * li * dt)
    a16_ref[:, S5_ST:] = jnp.exp(S5_L * lr * dt) * jnp.sin(S5_L * li * dt)

    lrc, lic = lr_c[...], li_c[...]
    dtc = jnp.exp(ls_c[...])
    e_out = lax.broadcasted_iota(jnp.int32, (1, S5_L), 1).astype(F32) + 1.0
    pm2 = jnp.exp(e_out * (lrc * dtc))
    ar2 = pm2 * jnp.cos(e_out * (lic * dtc))
    ai2 = pm2 * jnp.sin(e_out * (lic * dtc))
    own2 = (lax.broadcasted_iota(jnp.int32, (S5_ST, LANES), 0) // SSM_STATE
            == lax.broadcasted_iota(jnp.int32, (S5_ST, LANES), 1) // SSM_GROUP)
    cr = jnp.where(own2, cxr_ref[...], 0.0)
    ci = jnp.where(own2, cxi_ref[...], 0.0)
    for t in range(S5_L):
        art, ait = ar2[:, t:t + 1], ai2[:, t:t + 1]
        cols = slice(t * LANES, (t + 1) * LANES)
        wout_ref[:S5_ST, cols] = (cr * art - ci * ait).astype(BF16)
        wout_ref[S5_ST:, cols] = (-(cr * ait + ci * art)).astype(BF16)
    z = _dot(wsin_ref[...], jnp.concatenate([cr, -ci], axis=0).astype(BF16))
    z_ref[...] = z.astype(BF16)


def _s5_prep(lam_re, lam_im, log_step, b_re, b_im, c_re, c_im):
    n = DEPTH * S5_NB
    blk = lambda a: a.reshape((n, S5_GL) + a.shape[2:])
    ls = jnp.broadcast_to(log_step[..., None], (DEPTH, SSM_GROUPS, SSM_STATE))
    rows = [blk(a).reshape(n, 1, S5_ST) for a in (lam_re, lam_im, ls)]
    cols = [blk(a).reshape(n, S5_ST, 1) for a in (lam_re, lam_im, ls)]
    b16 = [jnp.transpose(blk(a), (0, 3, 1, 2)).reshape(n, SSM_GROUP, S5_ST) for a in (b_re, b_im)]
    cx = [jnp.tile(jnp.swapaxes(blk(a), 2, 3).reshape(n, S5_ST, SSM_GROUP), (1, 1, S5_GL))
          for a in (c_re, c_im)]
    spec = lambda *tail: pl.BlockSpec((None,) + tail, lambda i: (i,) + (0,) * len(tail))
    outs = [((S5_IN, 2 * S5_ST), BF16), ((2 * S5_ST, S5_IN), BF16), ((S5_IN, LANES), BF16),
            ((1, 2 * S5_ST), F32)]
    return pl.pallas_call(
        _s5_prep_kernel,
        grid=(n,),
        in_specs=([spec(1, S5_ST)] * 3 + [spec(S5_ST, 1)] * 3 + [spec(SSM_GROUP, S5_ST)] * 2
                  + [spec(S5_ST, LANES)] * 2),
        out_specs=[spec(*s) for s, _ in outs],
        out_shape=[jax.ShapeDtypeStruct((n,) + s, t) for s, t in outs],
        compiler_params=pltpu.CompilerParams(
            dimension_semantics=("arbitrary",), vmem_limit_bytes=VMEM_LIMIT),
        name="s5_prep",
    )(*rows, *cols, *b16, *cx)


S5_CT = 32
S5_TB = 2 * LANES


def _s5_kernel(u_ref, wsin_ref, z_ref, wout_ref, a16_ref, dsk_ref, y_ref,
               x_scr, sin_scr, xp_scr, carry_scr, wintra_ref):
    rows = BATCH * S5_CT

    @pl.when(pl.program_id(1) == 0)
    def _():
        carry_scr[...] = jnp.zeros_like(carry_scr)
        for t in range(S5_L):
            keep = (t + 1) * LANES
            cols = slice(t * LANES, (t + 1) * LANES)
            wintra_ref[:keep, cols] = z_ref[S5_IN - keep:, :]
            if keep < S5_IN:
                wintra_ref[keep:, cols] = jnp.zeros((S5_IN - keep, LANES), BF16)

    for s in range(S5_L):
        xs = u_ref[:, pl.ds(s, S5_CT, stride=S5_L), :]
        x_scr[:, s * LANES:(s + 1) * LANES] = xs.reshape(rows, LANES).astype(BF16)

    nt = S5_ST // LANES
    tile = lambda k: slice(k * LANES, (k + 1) * LANES)
    sin = _dot(x_scr[...], wsin_ref[...])
    for k in range(2 * nt):
        sin_scr[k] = sin[:, tile(k)]
    a_re = [a16_ref[:, tile(k)] for k in range(nt)]
    a_im = [a16_ref[:, tile(nt + k)] for k in range(nt)]

    def step(c, carry):
        at_c = pl.ds(c, BATCH, stride=S5_CT)
        nxt_re, nxt_im = [], []
        for k in range(nt):
            xr, xi = carry[k], carry[nt + k]
            xp_scr[k, at_c, :] = xr
            xp_scr[nt + k, at_c, :] = xi
            nxt_re.append(a_re[k] * xr - a_im[k] * xi + sin_scr[k, at_c, :])
            nxt_im.append(a_re[k] * xi + a_im[k] * xr + sin_scr[nt + k, at_c, :])
        return tuple(nxt_re + nxt_im)

    state = lax.fori_loop(0, S5_CT, step, tuple(carry_scr[:, tile(k)] for k in range(2 * nt)),
                          unroll=4)
    for k in range(2 * nt):
        carry_scr[:, tile(k)] = state[k]

    xp = jnp.concatenate([xp_scr[k] for k in range(2 * nt)], axis=-1).astype(BF16)
    for j in range(S5_IN // S5_TB):
        cols = slice(j * S5_TB, (j + 1) * S5_TB)
        kend = (j + 1) * S5_TB
        y = (_dot(xp, wout_ref[:, cols]) + _dot(x_scr[:, :kend], wintra_ref[:kend, cols])
             + dsk_ref[:, cols] * x_scr[:, cols].astype(F32))
        g = jax.nn.gelu(y)
        for tt in range(S5_TB // LANES):
            t = j * (S5_TB // LANES) + tt
            y_ref[:, pl.ds(t, S5_CT, stride=S5_L), :] = (
                g[:, tt * LANES:(tt + 1) * LANES].reshape(BATCH, S5_CT, LANES))


def _s5(u, layer, wsin, wout, z, a16, dsk):
    rows = S5_CT * BATCH
    wspec = lambda a: pl.BlockSpec((None,) + a.shape[1:], lambda g, c: (layer * S5_NB + g, 0, 0))
    xspec = pl.BlockSpec((None, BATCH, S5_CT * S5_L, LANES), lambda g, c: (g, 0, c, 0))
    return pl.pallas_call(
        _s5_kernel,
        grid=(S5_NB, S5_CHUNKS // S5_CT),
        in_specs=[xspec, wspec(wsin), wspec(z), wspec(wout), wspec(a16), wspec(dsk)],
        out_specs=xspec,
        out_shape=jax.ShapeDtypeStruct(u.shape, F32),
        scratch_shapes=[pltpu.VMEM((rows, S5_IN), BF16),
                        pltpu.VMEM((2 * S5_ST // LANES, rows, LANES), F32),
                        pltpu.VMEM((2 * S5_ST // LANES, rows, LANES), F32),
                        pltpu.VMEM((BATCH, 2 * S5_ST), F32), pltpu.VMEM((S5_IN, S5_IN), BF16)],
        compiler_params=pltpu.CompilerParams(
            dimension_semantics=("arbitrary", "arbitrary"), vmem_limit_bytes=VMEM_LIMIT),
        name="s5",
    )(u, wsin, z, wout, a16, dsk)


def _mix_out_kernel(x_ref, o_ref, yg_ref, ga_ref, gs_ref, woa_ref, wglu_ref, wout_ref, gpost_ref,
                    xo_ref):
    y_att = _dot(o_ref[...], woa_ref[...])
    yg = jnp.concatenate([yg_ref[gb] for gb in range(S5_NB)], axis=-1).astype(BF16)
    y_ssm = _dot(yg, wglu_ref[:, :D_MODEL]) * jax.nn.sigmoid(_dot(yg, wglu_ref[:, D_MODEL:]))
    merged = ga_ref[...].astype(F32) * y_att + gs_ref[...].astype(F32) * y_ssm
    m = _dot(merged.astype(BF16), wout_ref[...])
    xo_ref[...] = x_ref[...] + _rms(m, gpost_ref[...])


def _mix_out(x, layer, o, yg, ga, gs, woa, wglu, wout, gpost, tm=512):
    row = lambda d: pl.BlockSpec((None, tm, d), lambda b, i: (b, i, 0))
    full = lambda a: _layer_spec(a, layer)
    yg_spec = pl.BlockSpec((S5_NB, None, tm, LANES), lambda b, i: (0, b, i, 0))
    return pl.pallas_call(
        _mix_out_kernel,
        grid=(BATCH, SEQ // tm),
        in_specs=[row(D_MODEL), row(N_HEADS * V_HEAD), yg_spec, row(D_MODEL), row(D_MODEL),
                  full(woa), full(wglu), full(wout), full(gpost)],
        out_specs=row(D_MODEL),
        out_shape=jax.ShapeDtypeStruct((BATCH, SEQ, D_MODEL), F32),
        compiler_params=pltpu.CompilerParams(
            dimension_semantics=("arbitrary", "arbitrary"), vmem_limit_bytes=VMEM_LIMIT),
        name="mix_out",
    )(x, o, yg, ga, gs, woa, wglu, wout, gpost)


FFN_HALO = 8


def _ffn_kernel(x_ref, gpre_ref, wup_ref, cw_ref, cb_ref, wdn_ref, gpost_ref, xo_ref, gate_scr):
    tm = x_ref.shape[0]
    i = pl.program_id(1)

    @pl.when(i == 0)
    def _():
        gate_scr[0:FFN_HALO, :] = jnp.zeros((FFN_HALO, D_FF), F32)

    @pl.when(i > 0)
    def _():
        gate_scr[0:FFN_HALO, :] = gate_scr[tm:tm + FFN_HALO, :]

    x = x_ref[...]
    h = _rms(x, gpre_ref[...]).astype(BF16)
    gate = _dot(h, wup_ref[:, :D_FF])
    val = _dot(h, wup_ref[:, D_FF:])
    gate_scr[FFN_HALO:, :] = gate
    conv = (cb_ref[...]
            + cw_ref[0:1, :] * gate_scr[pl.ds(FFN_HALO - 2, tm), :]
            + cw_ref[1:2, :] * gate_scr[pl.ds(FFN_HALO - 1, tm), :]
            + cw_ref[2:3, :] * gate)
    act = (jax.nn.gelu(conv) * val).astype(BF16)
    y = _dot(act, wdn_ref[...])
    xo_ref[...] = x + _rms(y, gpost_ref[...])


def _ffn(x, layer, gpre, wup, cw, cb, wdn, gpost, tm=512):
    row = pl.BlockSpec((None, tm, D_MODEL), lambda b, i: (b, i, 0))
    full = lambda a: _layer_spec(a, layer)
    return pl.pallas_call(
        _ffn_kernel,
        grid=(BATCH, SEQ // tm),
        in_specs=[row, full(gpre), full(wup), full(cw), full(cb), full(wdn), full(gpost)],
        out_specs=row,
        out_shape=jax.ShapeDtypeStruct((BATCH, SEQ, D_MODEL), F32),
        scratch_shapes=[pltpu.VMEM((tm + FFN_HALO, D_FF), F32)],
        compiler_params=pltpu.CompilerParams(
            dimension_semantics=("arbitrary", "arbitrary"), vmem_limit_bytes=VMEM_LIMIT),
        name="ffn",
    )(x, gpre, wup, cw, cb, wdn, gpost)


W_IN_ROWS = 256


def _w_in_layout_kernel(w_ref, o_ref):
    o1 = Q_LORA
    o2 = o1 + KV_LORA
    o3 = o2 + QK_ROPE
    o4 = o3 + SSM_WIDTH
    o_ref[:, C_Q0:C_KPE0] = w_ref[:, :o1].astype(BF16)
    o_ref[:, C_KPE0:C_KV0] = jnp.zeros((W_IN_ROWS, C_KV0 - C_KPE0), BF16)
    o_ref[:, C_KPE0:C_KPE0 + QK_ROPE] = w_ref[:, o2:o3].astype(BF16)
    o_ref[:, C_KV0:C_U0] = w_ref[:, o1:o2].astype(BF16)
    o_ref[:, C_U0:C_GA0] = w_ref[:, o3:o4].astype(BF16)
    o_ref[:, C_GA0:C_END] = w_ref[:, o4:].astype(BF16)


def _w_in_layout(w_in):
    cols = w_in.shape[-1]
    return pl.pallas_call(
        _w_in_layout_kernel,
        grid=(DEPTH, D_MODEL // W_IN_ROWS),
        in_specs=[pl.BlockSpec((None, W_IN_ROWS, cols), lambda l, r: (l, r, 0))],
        out_specs=pl.BlockSpec((None, W_IN_ROWS, C_END), lambda l, r: (l, r, 0)),
        out_shape=jax.ShapeDtypeStruct((DEPTH, D_MODEL, C_END), BF16),
        compiler_params=pltpu.CompilerParams(
            dimension_semantics=("arbitrary", "arbitrary"), vmem_limit_bytes=VMEM_LIMIT),
        name="w_in_layout",
    )(w_in)


def _layout_w_uq(w):
    w4 = w.reshape(DEPTH, Q_LORA, N_HEADS, QK_NOPE + QK_ROPE)
    pad = jnp.zeros((DEPTH, Q_LORA, N_HEADS, HEAD_PAD - QK_NOPE - QK_ROPE), w.dtype)
    return jnp.concatenate([w4, pad], axis=-1).reshape(DEPTH, Q_LORA, N_HEADS * HEAD_PAD).astype(BF16)


def _layout_w_ukv(w):
    w4 = w.reshape(DEPTH, KV_LORA, N_HEADS, QK_NOPE + V_HEAD)
    return (w4[..., :QK_NOPE].reshape(DEPTH, KV_LORA, N_HEADS * QK_NOPE).astype(BF16),
            w4[..., QK_NOPE:].reshape(DEPTH, KV_LORA, N_HEADS * V_HEAD).astype(BF16))


def _rope_tables():
    pos = jnp.arange(SEQ, dtype=F32)
    inv_freq = ROPE_THETA ** (-jnp.arange(0, QK_ROPE, 2, dtype=F32) / QK_ROPE)
    ang = pos[:, None] * inv_freq[None, :]
    cos, sin = jnp.cos(ang), jnp.sin(ang)
    scale = math.log2(math.e) / math.sqrt(QK_NOPE + QK_ROPE)
    ones = jnp.ones((SEQ, QK_NOPE), F32)
    zq = jnp.zeros((SEQ, HEAD_PAD - QK_NOPE - QK_ROPE), F32)
    cq = jnp.concatenate([ones, cos, cos, zq], axis=1) * scale
    sq = jnp.concatenate([0.0 * ones, -sin, sin, zq], axis=1) * scale
    zk = jnp.zeros((SEQ, LANES - QK_ROPE), F32)
    ck = jnp.concatenate([cos, cos, zk], axis=1)
    sk = jnp.concatenate([-sin, sin, zk], axis=1)
    return cq, sq, ck, sk


def kernel(x, w_in, b_gate, g_mix_pre, g_q, w_uq, g_kv, w_ukv, w_o_att, lam_re, lam_im, log_step, b_re, b_im, c_re, c_im, d_skip, w_glu, w_out, g_mix_post, g_ffn_pre, w_up, conv_w, conv_b, w_down, g_ffn_post):
    cq, sq, ck, sk = _rope_tables()
    wsin, wout_s5, z_s5, a16 = _s5_prep(lam_re, lam_im, log_step, b_re, b_im, c_re, c_im)
    dsk = jnp.tile(d_skip.reshape(DEPTH * S5_NB, 1, LANES), (1, 1, S5_L))
    cw = jnp.pad(conv_w, ((0, 0), (0, FFN_HALO - CONV_W), (0, 0)))
    vec = lambda a: a[:, None, :]
    bf = lambda a: a.astype(BF16)
    win = _w_in_layout(w_in)
    wuq = _layout_w_uq(w_uq)
    wkn, wv = _layout_w_ukv(w_ukv)
    woa, wglu, wout, wup, wdn = bf(w_o_att), bf(w_glu), bf(w_out), bf(w_up), bf(w_down)
    gpre, bg, gq, gkv, gpost = vec(g_mix_pre), vec(b_gate), vec(g_q), vec(g_kv), vec(g_mix_post)
    fpre, cb, fpost = vec(g_ffn_pre), vec(conv_b), vec(g_ffn_post)
    for l in range(DEPTH):
        q, k, v, u, ga, gs = _mix_in(x, l, gpre, win, bg, gq, wuq, gkv, wkn, wv, cq, sq, ck, sk)
        o = _attn(q, k, v)
        yg = _s5(u, l, wsin, wout_s5, z_s5, a16, dsk)
        x = _mix_out(x, l, o, yg, ga, gs, woa, wglu, wout, gpost)
        x = _ffn(x, l, fpre, wup, cw, cb, wdn, fpost)
    return x
```

```python
import math

import jax
import jax.numpy as jnp
from jax import lax
from jax.experimental import pallas as pl
from jax.experimental.pallas import tpu as pltpu

D_MODEL = 1024
BATCH = 16
SEQ = 2048
DEPTH = 4
CHUNK = 64
N_HEADS = 8
QK_NOPE = 64
QK_ROPE = 32
V_HEAD = 64
Q_LORA = 384
KV_LORA = 256
ROPE_THETA = 10000.0
SSM_WIDTH = 512
SSM_GROUP = 16
SSM_GROUPS = SSM_WIDTH // SSM_GROUP
SSM_STATE = 64
D_FF = 2816
CONV_W = 3
EPS = 1e-6

LANES = 128
HEAD_PAD = LANES
S5_L = 16
S5_CHUNKS = SEQ // S5_L
S5_GL = LANES // SSM_GROUP
S5_NB = SSM_GROUPS // S5_GL
S5_IN = S5_L * LANES
S5_ST = S5_GL * SSM_STATE
VMEM_LIMIT = 56 * 1024 * 1024

F32 = jnp.float32
BF16 = jnp.bfloat16


def _rms(x, g):
    var = jnp.mean(x * x, axis=-1, keepdims=True)
    return x * lax.rsqrt(var + EPS) * g


def _dot(a, b):
    return jnp.dot(a, b, preferred_element_type=F32)


def _dot_nt(a, b):
    return lax.dot_general(a, b, (((1,), (1,)), ((), ())), preferred_element_type=F32)


C_Q0, C_KPE0, C_KV0, C_U0, C_GA0, C_GS0, C_END = 0, 384, 512, 768, 1280, 2304, 3328
ROPE_HALF = QK_ROPE // 2


def _swap_halves(x, lo):
    lane = lax.broadcasted_iota(jnp.int32, x.shape, 1)
    first = (lane >= lo) & (lane < lo + ROPE_HALF)
    return jnp.where(first, pltpu.roll(x, LANES - ROPE_HALF, axis=1), pltpu.roll(x, ROPE_HALF, axis=1))


def _mix_in_kernel(x_ref, gpre_ref, win_ref, bg_ref, gq_ref, wuq_ref,
                   gkv_ref, wkn_ref, wv_ref, cq_ref, sq_ref, ck_ref, sk_ref,
                   q_ref, k_ref, v_ref, u_ref, ga_ref, gs_ref):
    h = _rms(x_ref[...], gpre_ref[...]).astype(BF16)

    z_q = _dot(h, win_ref[:, C_Q0:C_KV0])
    cqn = _rms(z_q[:, :Q_LORA], gq_ref[...]).astype(BF16)
    q = _dot(cqn, wuq_ref[...])
    cq, sq = cq_ref[...], sq_ref[...]
    for hd in range(N_HEADS):
        sl = slice(hd * HEAD_PAD, (hd + 1) * HEAD_PAD)
        qh = q[:, sl]
        q_ref[:, sl] = (qh * cq + _swap_halves(qh, QK_NOPE) * sq).astype(BF16)

    kpe = z_q[:, C_KPE0:C_KV0]
    kpe = pltpu.roll(kpe * ck_ref[...] + _swap_halves(kpe, 0) * sk_ref[...], QK_NOPE, axis=1)

    c_kv = _dot(h, win_ref[:, C_KV0:C_U0])
    ckvn = _rms(c_kv, gkv_ref[...]).astype(BF16)
    kn = _dot(ckvn, wkn_ref[...])
    lane = lax.broadcasted_iota(jnp.int32, kpe.shape, 1)
    for hp in range(N_HEADS // 2):
        pair = kn[:, hp * LANES:(hp + 1) * LANES]
        for hh, nope in enumerate((pair, pltpu.roll(pair, QK_NOPE, axis=1))):
            sl = slice((2 * hp + hh) * HEAD_PAD, (2 * hp + hh + 1) * HEAD_PAD)
            k_ref[:, sl] = jnp.where(lane < QK_NOPE, nope, kpe).astype(BF16)
    v_ref[...] = _dot(ckvn, wv_ref[...]).astype(BF16)

    u = _dot(h, win_ref[:, C_U0:C_GA0])
    for gb in range(S5_NB):
        u_ref[gb] = u[:, gb * LANES:(gb + 1) * LANES]
    ga_ref[...] = jax.nn.sigmoid(
        _dot(h, win_ref[:, C_GA0:C_GS0]) + bg_ref[:, :D_MODEL]).astype(BF16)
    gs_ref[...] = jax.nn.sigmoid(
        _dot(h, win_ref[:, C_GS0:C_END]) + bg_ref[:, D_MODEL:]).astype(BF16)


def _layer_spec(a, layer):
    return pl.BlockSpec((None,) + a.shape[1:], lambda *_: (layer,) + (0,) * (a.ndim - 1))


def _mix_in(x, layer, gpre, win, bg, gq, wuq, gkv, wkn, wv, cq, sq, ck, sk, tm=512):
    nt = SEQ // tm
    row = lambda d: pl.BlockSpec((None, tm, d), lambda i, b: (b, i, 0))
    full = lambda a: _layer_spec(a, layer)
    tab = pl.BlockSpec((tm, LANES), lambda i, b: (i, 0))
    hq = N_HEADS * HEAD_PAD
    outs = [(hq, BF16), (hq, BF16), (N_HEADS * V_HEAD, BF16), None, (D_MODEL, BF16), (D_MODEL, BF16)]
    u_spec = pl.BlockSpec((S5_NB, None, tm, LANES), lambda i, b: (0, b, i, 0))
    u_shape = jax.ShapeDtypeStruct((S5_NB, BATCH, SEQ, LANES), F32)
    return pl.pallas_call(
        _mix_in_kernel,
        grid=(nt, BATCH),
        in_specs=[row(D_MODEL), full(gpre), full(win), full(bg), full(gq), full(wuq),
                  full(gkv), full(wkn), full(wv), tab, tab, tab, tab],
        out_specs=[u_spec if o is None else row(o[0]) for o in outs],
        out_shape=[u_shape if o is None else jax.ShapeDtypeStruct((BATCH, SEQ, o[0]), o[1])
                   for o in outs],
        compiler_params=pltpu.CompilerParams(
            dimension_semantics=("arbitrary", "arbitrary"), vmem_limit_bytes=VMEM_LIMIT),
        name="mix_in",
    )(x, gpre, win, bg, gq, wuq, gkv, wkn, wv, cq, sq, ck, sk)


ATT_TQ = 256


def _attn_kernel(q_ref, k_ref, v_ref, o_ref):
    tq = ATT_TQ
    row_chunk = lax.broadcasted_iota(jnp.int32, (tq, tq), 0) // CHUNK
    col_chunk = lax.broadcasted_iota(jnp.int32, (tq, tq), 1) // CHUNK
    mask = row_chunk >= col_chunk
    lane = lax.broadcasted_iota(jnp.int32, (tq, 2 * V_HEAD), 1)
    for i in range(SEQ // tq):
        q0 = i * tq
        outs = []
        for hh in range(2):
            hs = slice(hh * HEAD_PAD, (hh + 1) * HEAD_PAD)
            qh = q_ref[q0:q0 + tq, hs]
            s_d = jnp.where(mask, _dot_nt(qh, k_ref[q0:q0 + tq, hs]), -jnp.inf)
            m = jnp.max(s_d, axis=-1, keepdims=True)
            if i > 0:
                s_p = _dot_nt(qh, k_ref[0:q0, hs])
                m = jnp.maximum(m, jnp.max(s_p, axis=-1, keepdims=True))
            p_d = jnp.exp2(s_d - m)
            l = jnp.sum(p_d, axis=-1, keepdims=True)
            acc = _dot(p_d.astype(BF16), v_ref[q0:q0 + tq, :])
            if i > 0:
                p_p = jnp.exp2(s_p - m)
                l = l + jnp.sum(p_p, axis=-1, keepdims=True)
                acc = acc + _dot(p_p.astype(BF16), v_ref[0:q0, :])
            outs.append(acc / l)
        o_ref[q0:q0 + tq, :] = jnp.where(lane < V_HEAD, outs[0], outs[1]).astype(BF16)


def _attn(q, k, v):
    blk = lambda d: pl.BlockSpec((None, SEQ, d), lambda b, p: (b, 0, p))
    return pl.pallas_call(
        _attn_kernel,
        grid=(BATCH, N_HEADS // 2),
        in_specs=[blk(2 * HEAD_PAD), blk(2 * HEAD_PAD), blk(2 * V_HEAD)],
        out_specs=blk(2 * V_HEAD),
        out_shape=jax.ShapeDtypeStruct((BATCH, SEQ, N_HEADS * V_HEAD), BF16),
        compiler_params=pltpu.CompilerParams(
            dimension_semantics=("arbitrary", "arbitrary"), vmem_limit_bytes=VMEM_LIMIT),
        name="attn",
    )(q, k, v)


def _s5_prep_kernel(lr_r, li_r, ls_r, lr_c, li_c, ls_c, b16r_ref, b16i_ref, cxr_ref, cxi_ref,
                    wsin_ref, wout_ref, z_ref, a16_ref):
    lr, li = lr_r[...], li_r[...]
    dt = jnp.exp(ls_r[...])
    mag = jnp.exp(lr * dt)
    abar_re = mag * jnp.cos(li * dt)
    abar_im = mag * jnp.sin(li * dt)
    nr, ni = abar_re - 1.0, abar_im
    den = lr * lr + li * li
    f_re = (nr * lr + ni * li) / den
    f_im = (ni * lr - nr * li) / den
    b16r, b16i = b16r_ref[...], b16i_ref[...]
    bbar_re = f_re * b16r - f_im * b16i
    bbar_im = f_re * b16i + f_im * b16r
    e_in = (S5_L - 1.0) - lax.broadcasted_iota(jnp.int32, (S5_L, 1), 0).astype(F32)
    pm = jnp.exp(e_in * (lr * dt))
    pr = pm * jnp.cos(e_in * (li * dt))
    pi = pm * jnp.sin(e_in * (li * dt))
    own = (lax.broadcasted_iota(jnp.int32, (LANES, S5_ST), 0) // SSM_GROUP
           == lax.broadcasted_iota(jnp.int32, (LANES, S5_ST), 1) // SSM_STATE)
    for s in range(S5_L):
        prs, pis = pr[s:s + 1, :], pi[s:s + 1, :]
        wre = jnp.concatenate([bbar_re * prs - bbar_im * pis] * S5_GL, axis=0)
        wim = jnp.concatenate([bbar_re * pis + bbar_im * prs] * S5_GL, axis=0)
        wre = jnp.where(own, wre, 0.0)
        wim = jnp.where(own, wim, 0.0)
        rows = slice(s * LANES, (s + 1) * LANES)
        wsin_ref[rows, :S5_ST] = wre.astype(BF16)
        wsin_ref[rows, S5_ST:] = wim.astype(BF16)
    a16_ref[:, :S5_ST] = jnp.exp(S5_L * lr * dt) * jnp.cos(S5_L * li * dt)
    a16_ref[:, S5_ST:] = jnp.exp(S5_L * lr * dt) * jnp.sin(S5_L * li * dt)

    lrc, lic = lr_c[...], li_c[...]
    dtc = jnp.exp(ls_c[...])
    e_out = lax.broadcasted_iota(jnp.int32, (1, S5_L), 1).astype(F32) + 1.0
    pm2 = jnp.exp(e_out * (lrc * dtc))
    ar2 = pm2 * jnp.cos(e_out * (lic * dtc))
    ai2 = pm2 * jnp.sin(e_out * (lic * dtc))
    own2 = (lax.broadcasted_iota(jnp.int32, (S5_ST, LANES), 0) // SSM_STATE
            == lax.broadcasted_iota(jnp.int32, (S5_ST, LANES), 1) // SSM_GROUP)
    cr = jnp.where(own2, cxr_ref[...], 0.0)
    ci = jnp.where(own2, cxi_ref[...], 0.0)
    for t in range(S5_L):
        art, ait = ar2[:, t:t + 1], ai2[:, t:t + 1]
        cols = slice(t * LANES, (t + 1) * LANES)
        wout_ref[:S5_ST, cols] = (cr * art - ci * ait).astype(BF16)
        wout_ref[S5_ST:, cols] = (-(cr * ait + ci * art)).astype(BF16)
    z = _dot(wsin_ref[...], jnp.concatenate([cr, -ci], axis=0).astype(BF16))
    z_ref[...] = z.astype(BF16)


def _s5_prep(lam_re, lam_im, log_step, b_re, b_im, c_re, c_im):
    n = DEPTH * S5_NB
    blk = lambda a: a.reshape((n, S5_GL) + a.shape[2:])
    ls = jnp.broadcast_to(log_step[..., None], (DEPTH, SSM_GROUPS, SSM_STATE))
    rows = [blk(a).reshape(n, 1, S5_ST) for a in (lam_re, lam_im, ls)]
    cols = [blk(a).reshape(n, S5_ST, 1) for a in (lam_re, lam_im, ls)]
    b16 = [jnp.transpose(blk(a), (0, 3, 1, 2)).reshape(n, SSM_GROUP, S5_ST) for a in (b_re, b_im)]
    cx = [jnp.tile(jnp.swapaxes(blk(a), 2, 3).reshape(n, S5_ST, SSM_GROUP), (1, 1, S5_GL))
          for a in (c_re, c_im)]
    spec = lambda *tail: pl.BlockSpec((None,) + tail, lambda i: (i,) + (0,) * len(tail))
    outs = [((S5_IN, 2 * S5_ST), BF16), ((2 * S5_ST, S5_IN), BF16), ((S5_IN, LANES), BF16),
            ((1, 2 * S5_ST), F32)]
    return pl.pallas_call(
        _s5_prep_kernel,
        grid=(n,),
        in_specs=([spec(1, S5_ST)] * 3 + [spec(S5_ST, 1)] * 3 + [spec(SSM_GROUP, S5_ST)] * 2
                  + [spec(S5_ST, LANES)] * 2),
        out_specs=[spec(*s) for s, _ in outs],
        out_shape=[jax.ShapeDtypeStruct((n,) + s, t) for s, t in outs],
        compiler_params=pltpu.CompilerParams(
            dimension_semantics=("arbitrary",), vmem_limit_bytes=VMEM_LIMIT),
        name="s5_prep",
    )(*rows, *cols, *b16, *cx)


S5_CT = 32
S5_PITCH = S5_CT + 8
S5_TB = 2 * LANES


def _s5_kernel(u_ref, wsin_ref, z_ref, wout_ref, a16_ref, dsk_ref, y_ref,
               x_scr, sin_scr, xp_scr, carry_scr, wintra_ref):
    rows = BATCH * S5_CT

    @pl.when(pl.program_id(1) == 0)
    def _():
        carry_scr[...] = jnp.zeros_like(carry_scr)
        for t in range(S5_L):
            keep = (t + 1) * LANES
            cols = slice(t * LANES, (t + 1) * LANES)
            wintra_ref[:keep, cols] = z_ref[S5_IN - keep:, :]
            if keep < S5_IN:
                wintra_ref[keep:, cols] = jnp.zeros((S5_IN - keep, LANES), BF16)

    for s in range(S5_L):
        xs = u_ref[:, pl.ds(s, S5_CT, stride=S5_L), :]
        x_scr[:, s * LANES:(s + 1) * LANES] = xs.reshape(rows, LANES).astype(BF16)

    nt = S5_ST // LANES
    tile = lambda k: slice(k * LANES, (k + 1) * LANES)
    batch_rows = lambda b: slice(b * S5_PITCH, b * S5_PITCH + S5_CT)
    sin = _dot(x_scr[...], wsin_ref[...])
    for k in range(2 * nt):
        for b in range(BATCH):
            sin_scr[k, batch_rows(b), :] = sin[b * S5_CT:(b + 1) * S5_CT, tile(k)]
    a_re = [a16_ref[:, tile(k)] for k in range(nt)]
    a_im = [a16_ref[:, tile(nt + k)] for k in range(nt)]

    def step(c, carry):
        at_c = pl.ds(c, BATCH, stride=S5_PITCH)
        nxt_re, nxt_im = [], []
        for k in range(nt):
            xr, xi = carry[k], carry[nt + k]
            xp_scr[k, at_c, :] = xr
            xp_scr[nt + k, at_c, :] = xi
            nxt_re.append(a_re[k] * xr - a_im[k] * xi + sin_scr[k, at_c, :])
            nxt_im.append(a_re[k] * xi + a_im[k] * xr + sin_scr[nt + k, at_c, :])
        return tuple(nxt_re + nxt_im)

    state = lax.fori_loop(0, S5_CT, step, tuple(carry_scr[:, tile(k)] for k in range(2 * nt)),
                          unroll=4)
    for k in range(2 * nt):
        carry_scr[:, tile(k)] = state[k]

    xp = jnp.concatenate(
        [jnp.concatenate([xp_scr[k, batch_rows(b), :] for b in range(BATCH)], axis=0)
         for k in range(2 * nt)], axis=-1).astype(BF16)
    for j in range(S5_IN // S5_TB):
        cols = slice(j * S5_TB, (j + 1) * S5_TB)
        kend = (j + 1) * S5_TB
        y = (_dot(xp, wout_ref[:, cols]) + _dot(x_scr[:, :kend], wintra_ref[:kend, cols])
             + dsk_ref[:, cols] * x_scr[:, cols].astype(F32))
        g = jax.nn.gelu(y)
        for tt in range(S5_TB // LANES):
            t = j * (S5_TB // LANES) + tt
            y_ref[:, pl.ds(t, S5_CT, stride=S5_L), :] = (
                g[:, tt * LANES:(tt + 1) * LANES].reshape(BATCH, S5_CT, LANES))


def _s5(u, layer, wsin, wout, z, a16, dsk):
    rows = S5_CT * BATCH
    wspec = lambda a: pl.BlockSpec((None,) + a.shape[1:], lambda g, c: (layer * S5_NB + g, 0, 0))
    xspec = pl.BlockSpec((None, BATCH, S5_CT * S5_L, LANES), lambda g, c: (g, 0, c, 0))
    return pl.pallas_call(
        _s5_kernel,
        grid=(S5_NB, S5_CHUNKS // S5_CT),
        in_specs=[xspec, wspec(wsin), wspec(z), wspec(wout), wspec(a16), wspec(dsk)],
        out_specs=xspec,
        out_shape=jax.ShapeDtypeStruct(u.shape, F32),
        scratch_shapes=[pltpu.VMEM((rows, S5_IN), BF16),
                        pltpu.VMEM((2 * S5_ST // LANES, BATCH * S5_PITCH, LANES), F32),
                        pltpu.VMEM((2 * S5_ST // LANES, BATCH * S5_PITCH, LANES), F32),
                        pltpu.VMEM((BATCH, 2 * S5_ST), F32), pltpu.VMEM((S5_IN, S5_IN), BF16)],
        compiler_params=pltpu.CompilerParams(
            dimension_semantics=("arbitrary", "arbitrary"), vmem_limit_bytes=VMEM_LIMIT),
        name="s5",
    )(u, wsin, z, wout, a16, dsk)


def _mix_out_kernel(x_ref, o_ref, yg_ref, ga_ref, gs_ref, woa_ref, wglu_ref, wout_ref, gpost_ref,
                    xo_ref):
    y_att = _dot(o_ref[...], woa_ref[...])
    yg = jnp.concatenate([yg_ref[gb] for gb in range(S5_NB)], axis=-1).astype(BF16)
    y_ssm = _dot(yg, wglu_ref[:, :D_MODEL]) * jax.nn.sigmoid(_dot(yg, wglu_ref[:, D_MODEL:]))
    merged = ga_ref[...].astype(F32) * y_att + gs_ref[...].astype(F32) * y_ssm
    m = _dot(merged.astype(BF16), wout_ref[...])
    xo_ref[...] = x_ref[...] + _rms(m, gpost_ref[...])


def _mix_out(x, layer, o, yg, ga, gs, woa, wglu, wout, gpost, tm=512):
    row = lambda d: pl.BlockSpec((None, tm, d), lambda b, i: (b, i, 0))
    full = lambda a: _layer_spec(a, layer)
    yg_spec = pl.BlockSpec((S5_NB, None, tm, LANES), lambda b, i: (0, b, i, 0))
    return pl.pallas_call(
        _mix_out_kernel,
        grid=(BATCH, SEQ // tm),
        in_specs=[row(D_MODEL), row(N_HEADS * V_HEAD), yg_spec, row(D_MODEL), row(D_MODEL),
                  full(woa), full(wglu), full(wout), full(gpost)],
        out_specs=row(D_MODEL),
        out_shape=jax.ShapeDtypeStruct((BATCH, SEQ, D_MODEL), F32),
        compiler_params=pltpu.CompilerParams(
            dimension_semantics=("arbitrary", "arbitrary"), vmem_limit_bytes=VMEM_LIMIT),
        name="mix_out",
    )(x, o, yg, ga, gs, woa, wglu, wout, gpost)


FFN_HALO = 8


def _ffn_kernel(x_ref, gpre_ref, wup_ref, cw_ref, cb_ref, wdn_ref, gpost_ref, xo_ref, gate_scr):
    tm = x_ref.shape[0]
    i = pl.program_id(1)

    @pl.when(i == 0)
    def _():
        gate_scr[0:FFN_HALO, :] = jnp.zeros((FFN_HALO, D_FF), F32)

    @pl.when(i > 0)
    def _():
        gate_scr[0:FFN_HALO, :] = gate_scr[tm:tm + FFN_HALO, :]

    x = x_ref[...]
    h = _rms(x, gpre_ref[...]).astype(BF16)
    gate = _dot(h, wup_ref[:, :D_FF])
    val = _dot(h, wup_ref[:, D_FF:])
    gate_scr[FFN_HALO:, :] = gate
    conv = (cb_ref[...]
            + cw_ref[0:1, :] * gate_scr[pl.ds(FFN_HALO - 2, tm), :]
            + cw_ref[1:2, :] * gate_scr[pl.ds(FFN_HALO - 1, tm), :]
            + cw_ref[2:3, :] * gate)
    act = (jax.nn.gelu(conv) * val).astype(BF16)
    y = _dot(act, wdn_ref[...])
    xo_ref[...] = x + _rms(y, gpost_ref[...])


def _ffn(x, layer, gpre, wup, cw, cb, wdn, gpost, tm=512):
    row = pl.BlockSpec((None, tm, D_MODEL), lambda b, i: (b, i, 0))
    full = lambda a: _layer_spec(a, layer)
    return pl.pallas_call(
        _ffn_kernel,
        grid=(BATCH, SEQ // tm),
        in_specs=[row, full(gpre), full(wup), full(cw), full(cb), full(wdn), full(gpost)],
        out_specs=row,
        out_shape=jax.ShapeDtypeStruct((BATCH, SEQ, D_MODEL), F32),
        scratch_shapes=[pltpu.VMEM((tm + FFN_HALO, D_FF), F32)],
        compiler_params=pltpu.CompilerParams(
            dimension_semantics=("arbitrary", "arbitrary"), vmem_limit_bytes=VMEM_LIMIT),
        name="ffn",
    )(x, gpre, wup, cw, cb, wdn, gpost)


W_IN_ROWS = 256


def _w_in_layout_kernel(w_ref, o_ref):
    o1 = Q_LORA
    o2 = o1 + KV_LORA
    o3 = o2 + QK_ROPE
    o4 = o3 + SSM_WIDTH
    o_ref[:, C_Q0:C_KPE0] = w_ref[:, :o1].astype(BF16)
    o_ref[:, C_KPE0:C_KV0] = jnp.zeros((W_IN_ROWS, C_KV0 - C_KPE0), BF16)
    o_ref[:, C_KPE0:C_KPE0 + QK_ROPE] = w_ref[:, o2:o3].astype(BF16)
    o_ref[:, C_KV0:C_U0] = w_ref[:, o1:o2].astype(BF16)
    o_ref[:, C_U0:C_GA0] = w_ref[:, o3:o4].astype(BF16)
    o_ref[:, C_GA0:C_END] = w_ref[:, o4:].astype(BF16)


def _w_in_layout(w_in):
    cols = w_in.shape[-1]
    return pl.pallas_call(
        _w_in_layout_kernel,
        grid=(DEPTH, D_MODEL // W_IN_ROWS),
        in_specs=[pl.BlockSpec((None, W_IN_ROWS, cols), lambda l, r: (l, r, 0))],
        out_specs=pl.BlockSpec((None, W_IN_ROWS, C_END), lambda l, r: (l, r, 0)),
        out_shape=jax.ShapeDtypeStruct((DEPTH, D_MODEL, C_END), BF16),
        compiler_params=pltpu.CompilerParams(
            dimension_semantics=("arbitrary", "arbitrary"), vmem_limit_bytes=VMEM_LIMIT),
        name="w_in_layout",
    )(w_in)


def _cast_kernel(w_ref, o_ref):
    o_ref[...] = w_ref[...].astype(BF16)


def _to_bf16(w):
    _, rows, cols = w.shape
    spec = pl.BlockSpec((None, W_IN_ROWS, cols), lambda l, r: (l, r, 0))
    return pl.pallas_call(
        _cast_kernel,
        grid=(DEPTH, rows // W_IN_ROWS),
        in_specs=[spec],
        out_specs=spec,
        out_shape=jax.ShapeDtypeStruct(w.shape, BF16),
        compiler_params=pltpu.CompilerParams(
            dimension_semantics=("arbitrary", "arbitrary"), vmem_limit_bytes=VMEM_LIMIT),
        name="to_bf16",
    )(w)


def _layout_w_uq(w):
    w4 = w.reshape(DEPTH, Q_LORA, N_HEADS, QK_NOPE + QK_ROPE)
    pad = jnp.zeros((DEPTH, Q_LORA, N_HEADS, HEAD_PAD - QK_NOPE - QK_ROPE), w.dtype)
    return jnp.concatenate([w4, pad], axis=-1).reshape(DEPTH, Q_LORA, N_HEADS * HEAD_PAD).astype(BF16)


def _layout_w_ukv(w):
    w4 = w.reshape(DEPTH, KV_LORA, N_HEADS, QK_NOPE + V_HEAD)
    return (w4[..., :QK_NOPE].reshape(DEPTH, KV_LORA, N_HEADS * QK_NOPE).astype(BF16),
            w4[..., QK_NOPE:].reshape(DEPTH, KV_LORA, N_HEADS * V_HEAD).astype(BF16))


def _rope_tables():
    pos = jnp.arange(SEQ, dtype=F32)
    inv_freq = ROPE_THETA ** (-jnp.arange(0, QK_ROPE, 2, dtype=F32) / QK_ROPE)
    ang = pos[:, None] * inv_freq[None, :]
    cos, sin = jnp.cos(ang), jnp.sin(ang)
    scale = math.log2(math.e) / math.sqrt(QK_NOPE + QK_ROPE)
    ones = jnp.ones((SEQ, QK_NOPE), F32)
    zq = jnp.zeros((SEQ, HEAD_PAD - QK_NOPE - QK_ROPE), F32)
    cq = jnp.concatenate([ones, cos, cos, zq], axis=1) * scale
    sq = jnp.concatenate([0.0 * ones, -sin, sin, zq], axis=1) * scale
    zk = jnp.zeros((SEQ, LANES - QK_ROPE), F32)
    ck = jnp.concatenate([cos, cos, zk], axis=1)
    sk = jnp.concatenate([-sin, sin, zk], axis=1)
    return cq, sq, ck, sk


def kernel(x, w_in, b_gate, g_mix_pre, g_q, w_uq, g_kv, w_ukv, w_o_att, lam_re, lam_im, log_step, b_re, b_im, c_re, c_im, d_skip, w_glu, w_out, g_mix_post, g_ffn_pre, w_up, conv_w, conv_b, w_down, g_ffn_post):
    cq, sq, ck, sk = _rope_tables()
    wsin, wout_s5, z_s5, a16 = _s5_prep(lam_re, lam_im, log_step, b_re, b_im, c_re, c_im)
    dsk = jnp.tile(d_skip.reshape(DEPTH * S5_NB, 1, LANES), (1, 1, S5_L))
    cw = jnp.pad(conv_w, ((0, 0), (0, FFN_HALO - CONV_W), (0, 0)))
    vec = lambda a: a[:, None, :]
    win = _w_in_layout(w_in)
    wuq = _layout_w_uq(w_uq)
    wkn, wv = _layout_w_ukv(w_ukv)
    woa, wglu, wout, wup, wdn = map(_to_bf16, (w_o_att, w_glu, w_out, w_up, w_down))
    gpre, bg, gq, gkv, gpost = vec(g_mix_pre), vec(b_gate), vec(g_q), vec(g_kv), vec(g_mix_post)
    fpre, cb, fpost = vec(g_ffn_pre), vec(conv_b), vec(g_ffn_post)
    for l in range(DEPTH):
        q, k, v, u, ga, gs = _mix_in(x, l, gpre, win, bg, gq, wuq, gkv, wkn, wv, cq, sq, ck, sk)
        o = _attn(q, k, v)
        yg = _s5(u, l, wsin, wout_s5, z_s5, a16, dsk)
        x = _mix_out(x, l, o, yg, ga, gs, woa, wglu, wout, gpost)
        x = _ffn(x, l, fpre, wup, cw, cb, wdn, fpost)
    return x
```

```python
import math

import jax
import jax.numpy as jnp
from jax import lax
from jax.experimental import pallas as pl
from jax.experimental.pallas import tpu as pltpu

D_MODEL = 1024
BATCH = 16
SEQ = 2048
DEPTH = 4
CHUNK = 64
N_HEADS = 8
QK_NOPE = 64
QK_ROPE = 32
V_HEAD = 64
Q_LORA = 384
KV_LORA = 256
ROPE_THETA = 10000.0
SSM_WIDTH = 512
SSM_GROUP = 16
SSM_GROUPS = SSM_WIDTH // SSM_GROUP
SSM_STATE = 64
D_FF = 2816
CONV_W = 3
EPS = 1e-6

LANES = 128
HEAD_PAD = LANES
S5_L = 16
S5_CHUNKS = SEQ // S5_L
S5_GL = LANES // SSM_GROUP
S5_NB = SSM_GROUPS // S5_GL
S5_IN = S5_L * LANES
S5_ST = S5_GL * SSM_STATE
VMEM_LIMIT = 56 * 1024 * 1024

F32 = jnp.float32
BF16 = jnp.bfloat16


def _rms(x, g):
    var = jnp.mean(x * x, axis=-1, keepdims=True)
    return x * lax.rsqrt(var + EPS) * g


def _dot(a, b):
    return jnp.dot(a, b, preferred_element_type=F32)


def _dot_nt(a, b):
    return lax.dot_general(a, b, (((1,), (1,)), ((), ())), preferred_element_type=F32)


C_Q0, C_KPE0, C_KV0, C_U0, C_GA0, C_GS0, C_END = 0, 384, 512, 768, 1280, 2304, 3328
ROPE_HALF = QK_ROPE // 2


def _swap_halves(x, lo):
    lane = lax.broadcasted_iota(jnp.int32, x.shape, 1)
    first = (lane >= lo) & (lane < lo + ROPE_HALF)
    return jnp.where(first, pltpu.roll(x, LANES - ROPE_HALF, axis=1), pltpu.roll(x, ROPE_HALF, axis=1))


def _mix_in_kernel(x_ref, gpre_ref, win_ref, bg_ref, gq_ref, wuq_ref,
                   gkv_ref, wkn_ref, wv_ref, cq_ref, sq_ref, ck_ref, sk_ref,
                   q_ref, k_ref, v_ref, u_ref, ga_ref, gs_ref):
    h = _rms(x_ref[...], gpre_ref[...]).astype(BF16)

    z_q = _dot(h, win_ref[:, C_Q0:C_KV0])
    c_kv = _dot(h, win_ref[:, C_KV0:C_U0])

    u = _dot(h, win_ref[:, C_U0:C_GA0])
    for gb in range(S5_NB):
        u_ref[gb] = u[:, gb * LANES:(gb + 1) * LANES]

    cqn = _rms(z_q[:, :Q_LORA], gq_ref[...]).astype(BF16)
    ckvn = _rms(c_kv, gkv_ref[...]).astype(BF16)
    ga_ref[...] = jax.nn.sigmoid(
        _dot(h, win_ref[:, C_GA0:C_GS0]) + bg_ref[:, :D_MODEL]).astype(BF16)

    q = _dot(cqn, wuq_ref[...])
    kn = _dot(ckvn, wkn_ref[...])
    v_ref[...] = _dot(ckvn, wv_ref[...]).astype(BF16)
    gs_ref[...] = jax.nn.sigmoid(
        _dot(h, win_ref[:, C_GS0:C_END]) + bg_ref[:, D_MODEL:]).astype(BF16)

    cq, sq = cq_ref[...], sq_ref[...]
    for hd in range(N_HEADS):
        sl = slice(hd * HEAD_PAD, (hd + 1) * HEAD_PAD)
        qh = q[:, sl]
        q_ref[:, sl] = (qh * cq + _swap_halves(qh, QK_NOPE) * sq).astype(BF16)

    kpe = z_q[:, C_KPE0:C_KV0]
    kpe = pltpu.roll(kpe * ck_ref[...] + _swap_halves(kpe, 0) * sk_ref[...], QK_NOPE, axis=1)
    lane = lax.broadcasted_iota(jnp.int32, kpe.shape, 1)
    for hp in range(N_HEADS // 2):
        pair = kn[:, hp * LANES:(hp + 1) * LANES]
        for hh, nope in enumerate((pair, pltpu.roll(pair, QK_NOPE, axis=1))):
            sl = slice((2 * hp + hh) * HEAD_PAD, (2 * hp + hh + 1) * HEAD_PAD)
            k_ref[:, sl] = jnp.where(lane < QK_NOPE, nope, kpe).astype(BF16)


def _layer_spec(a, layer):
    return pl.BlockSpec((None,) + a.shape[1:], lambda *_: (layer,) + (0,) * (a.ndim - 1))


def _mix_in(x, layer, gpre, win, bg, gq, wuq, gkv, wkn, wv, cq, sq, ck, sk, tm=512):
    nt = SEQ // tm
    row = lambda d: pl.BlockSpec((None, tm, d), lambda i, b: (b, i, 0))
    full = lambda a: _layer_spec(a, layer)
    tab = pl.BlockSpec((tm, LANES), lambda i, b: (i, 0))
    hq = N_HEADS * HEAD_PAD
    outs = [(hq, BF16), (hq, BF16), (N_HEADS * V_HEAD, BF16), None, (D_MODEL, BF16), (D_MODEL, BF16)]
    u_spec = pl.BlockSpec((S5_NB, None, tm, LANES), lambda i, b: (0, b, i, 0))
    u_shape = jax.ShapeDtypeStruct((S5_NB, BATCH, SEQ, LANES), F32)
    return pl.pallas_call(
        _mix_in_kernel,
        grid=(nt, BATCH),
        in_specs=[row(D_MODEL), full(gpre), full(win), full(bg), full(gq), full(wuq),
                  full(gkv), full(wkn), full(wv), tab, tab, tab, tab],
        out_specs=[u_spec if o is None else row(o[0]) for o in outs],
        out_shape=[u_shape if o is None else jax.ShapeDtypeStruct((BATCH, SEQ, o[0]), o[1])
                   for o in outs],
        compiler_params=pltpu.CompilerParams(
            dimension_semantics=("arbitrary", "arbitrary"), vmem_limit_bytes=VMEM_LIMIT),
        name="mix_in",
    )(x, gpre, win, bg, gq, wuq, gkv, wkn, wv, cq, sq, ck, sk)


ATT_TQ = 256
ATT_SKEW = 1


def _attn_kernel(q_ref, k_ref, v_ref, o_ref):
    tq = ATT_TQ
    row_chunk = lax.broadcasted_iota(jnp.int32, (tq, tq), 0) // CHUNK
    col_chunk = lax.broadcasted_iota(jnp.int32, (tq, tq), 1) // CHUNK
    mask = row_chunk >= col_chunk
    lane = lax.broadcasted_iota(jnp.int32, (tq, 2 * V_HEAD), 1)

    def scores(i, hh):
        q0 = i * tq
        hs = slice(hh * HEAD_PAD, (hh + 1) * HEAD_PAD)
        qh = q_ref[q0:q0 + tq, hs]
        s_d = jnp.where(mask, _dot_nt(qh, k_ref[q0:q0 + tq, hs]), -jnp.inf)
        s_p = _dot_nt(qh, k_ref[0:q0, hs]) if i > 0 else None
        return s_d, s_p

    def softmax(s_d, s_p):
        m = jnp.max(s_d, axis=-1, keepdims=True)
        if s_p is not None:
            m = jnp.maximum(m, jnp.max(s_p, axis=-1, keepdims=True))
        p_d = jnp.exp2(s_d - m)
        l = jnp.sum(p_d, axis=-1, keepdims=True)
        p_p = None
        if s_p is not None:
            p_p = jnp.exp2(s_p - m)
            l = l + jnp.sum(p_p, axis=-1, keepdims=True)
            p_p = p_p.astype(BF16)
        return p_d.astype(BF16), p_p, l

    def weighted_values(i, p_d, p_p, l):
        q0 = i * tq
        acc = _dot(p_d, v_ref[q0:q0 + tq, :])
        if p_p is not None:
            acc = acc + _dot(p_p, v_ref[0:q0, :])
        return acc / l

    blocks = [(i, hh) for i in range(SEQ // tq) for hh in range(2)]
    sc, sm, outs = {}, {}, {}
    for n in range(len(blocks) + 2 * ATT_SKEW):
        if n < len(blocks):
            sc[n] = scores(*blocks[n])
        if ATT_SKEW <= n < len(blocks) + ATT_SKEW:
            sm[n - ATT_SKEW] = softmax(*sc.pop(n - ATT_SKEW))
        if n >= 2 * ATT_SKEW:
            i, hh = blocks[n - 2 * ATT_SKEW]
            outs[hh] = weighted_values(i, *sm.pop(n - 2 * ATT_SKEW))
            if hh == 1:
                o_ref[i * tq:(i + 1) * tq, :] = jnp.where(lane < V_HEAD, outs[0], outs[1]).astype(BF16)


def _attn(q, k, v):
    blk = lambda d: pl.BlockSpec((None, SEQ, d), lambda b, p: (b, 0, p))
    return pl.pallas_call(
        _attn_kernel,
        grid=(BATCH, N_HEADS // 2),
        in_specs=[blk(2 * HEAD_PAD), blk(2 * HEAD_PAD), blk(2 * V_HEAD)],
        out_specs=blk(2 * V_HEAD),
        out_shape=jax.ShapeDtypeStruct((BATCH, SEQ, N_HEADS * V_HEAD), BF16),
        compiler_params=pltpu.CompilerParams(
            dimension_semantics=("arbitrary", "arbitrary"), vmem_limit_bytes=VMEM_LIMIT),
        name="attn",
    )(q, k, v)


def _s5_prep_kernel(lr_r, li_r, ls_r, lr_c, li_c, ls_c, b16r_ref, b16i_ref, cxr_ref, cxi_ref,
                    wsin_ref, wout_ref, z_ref, a16_ref):
    lr, li = lr_r[...], li_r[...]
    dt = jnp.exp(ls_r[...])
    mag = jnp.exp(lr * dt)
    abar_re = mag * jnp.cos(li * dt)
    abar_im = mag * jnp.sin(li * dt)
    nr, ni = abar_re - 1.0, abar_im
    den = lr * lr + li * li
    f_re = (nr * lr + ni * li) / den
    f_im = (ni * lr - nr * li) / den
    b16r, b16i = b16r_ref[...], b16i_ref[...]
    bbar_re = f_re * b16r - f_im * b16i
    bbar_im = f_re * b16i + f_im * b16r
    e_in = (S5_L - 1.0) - lax.broadcasted_iota(jnp.int32, (S5_L, 1), 0).astype(F32)
    pm = jnp.exp(e_in * (lr * dt))
    pr = pm * jnp.cos(e_in * (li * dt))
    pi = pm * jnp.sin(e_in * (li * dt))
    own = (lax.broadcasted_iota(jnp.int32, (LANES, S5_ST), 0) // SSM_GROUP
           == lax.broadcasted_iota(jnp.int32, (LANES, S5_ST), 1) // SSM_STATE)
    for s in range(S5_L):
        prs, pis = pr[s:s + 1, :], pi[s:s + 1, :]
        wre = jnp.concatenate([bbar_re * prs - bbar_im * pis] * S5_GL, axis=0)
        wim = jnp.concatenate([bbar_re * pis + bbar_im * prs] * S5_GL, axis=0)
        wre = jnp.where(own, wre, 0.0)
        wim = jnp.where(own, wim, 0.0)
        rows = slice(s * LANES, (s + 1) * LANES)
        wsin_ref[rows, :S5_ST] = wre.astype(BF16)
        wsin_ref[rows, S5_ST:] = wim.astype(BF16)
    a16_ref[:, :S5_ST] = jnp.exp(S5_L * lr * dt) * jnp.cos(S5_L * li * dt)
    a16_ref[:, S5_ST:] = jnp.exp(S5_L * lr * dt) * jnp.sin(S5_L * li * dt)

    lrc, lic = lr_c[...], li_c[...]
    dtc = jnp.exp(ls_c[...])
    e_out = lax.broadcasted_iota(jnp.int32, (1, S5_L), 1).astype(F32) + 1.0
    pm2 = jnp.exp(e_out * (lrc * dtc))
    ar2 = pm2 * jnp.cos(e_out * (lic * dtc))
    ai2 = pm2 * jnp.sin(e_out * (lic * dtc))
    own2 = (lax.broadcasted_iota(jnp.int32, (S5_ST, LANES), 0) // SSM_STATE
            == lax.broadcasted_iota(jnp.int32, (S5_ST, LANES), 1) // SSM_GROUP)
    cr = jnp.where(own2, cxr_ref[...], 0.0)
    ci = jnp.where(own2, cxi_ref[...], 0.0)
    for t in range(S5_L):
        art, ait = ar2[:, t:t + 1], ai2[:, t:t + 1]
        cols = slice(t * LANES, (t + 1) * LANES)
        wout_ref[:S5_ST, cols] = (cr * art - ci * ait).astype(BF16)
        wout_ref[S5_ST:, cols] = (-(cr * ait + ci * art)).astype(BF16)
    z = _dot(wsin_ref[...], jnp.concatenate([cr, -ci], axis=0).astype(BF16))
    z_ref[...] = z.astype(BF16)


def _s5_prep(lam_re, lam_im, log_step, b_re, b_im, c_re, c_im):
    n = DEPTH * S5_NB
    blk = lambda a: a.reshape((n, S5_GL) + a.shape[2:])
    ls = jnp.broadcast_to(log_step[..., None], (DEPTH, SSM_GROUPS, SSM_STATE))
    rows = [blk(a).reshape(n, 1, S5_ST) for a in (lam_re, lam_im, ls)]
    cols = [blk(a).reshape(n, S5_ST, 1) for a in (lam_re, lam_im, ls)]
    b16 = [jnp.transpose(blk(a), (0, 3, 1, 2)).reshape(n, SSM_GROUP, S5_ST) for a in (b_re, b_im)]
    cx = [jnp.tile(jnp.swapaxes(blk(a), 2, 3).reshape(n, S5_ST, SSM_GROUP), (1, 1, S5_GL))
          for a in (c_re, c_im)]
    spec = lambda *tail: pl.BlockSpec((None,) + tail, lambda i: (i,) + (0,) * len(tail))
    outs = [((S5_IN, 2 * S5_ST), BF16), ((2 * S5_ST, S5_IN), BF16), ((S5_IN, LANES), BF16),
            ((1, 2 * S5_ST), F32)]
    return pl.pallas_call(
        _s5_prep_kernel,
        grid=(n,),
        in_specs=([spec(1, S5_ST)] * 3 + [spec(S5_ST, 1)] * 3 + [spec(SSM_GROUP, S5_ST)] * 2
                  + [spec(S5_ST, LANES)] * 2),
        out_specs=[spec(*s) for s, _ in outs],
        out_shape=[jax.ShapeDtypeStruct((n,) + s, t) for s, t in outs],
        compiler_params=pltpu.CompilerParams(
            dimension_semantics=("arbitrary",), vmem_limit_bytes=VMEM_LIMIT),
        name="s5_prep",
    )(*rows, *cols, *b16, *cx)


S5_CT = 32
S5_PITCH = S5_CT + 8
S5_TB = 2 * LANES


def _s5_kernel(u_ref, wsin_ref, z_ref, wout_ref, a16_ref, dsk_ref, y_ref,
               x_scr, sin_scr, xp_scr, carry_scr, wintra_ref):
    rows = BATCH * S5_CT

    @pl.when(pl.program_id(1) == 0)
    def _():
        carry_scr[...] = jnp.zeros_like(carry_scr)
        for t in range(S5_L):
            keep = (t + 1) * LANES
            cols = slice(t * LANES, (t + 1) * LANES)
            wintra_ref[:keep, cols] = z_ref[S5_IN - keep:, :]
            if keep < S5_IN:
                wintra_ref[keep:, cols] = jnp.zeros((S5_IN - keep, LANES), BF16)

    for s in range(S5_L):
        xs = u_ref[:, pl.ds(s, S5_CT, stride=S5_L), :]
        x_scr[:, s * LANES:(s + 1) * LANES] = xs.reshape(rows, LANES).astype(BF16)

    nt = S5_ST // LANES
    tile = lambda k: slice(k * LANES, (k + 1) * LANES)
    batch_rows = lambda b: slice(b * S5_PITCH, b * S5_PITCH + S5_CT)
    sin = _dot(x_scr[...], wsin_ref[...])
    for k in range(2 * nt):
        for b in range(BATCH):
            sin_scr[k, batch_rows(b), :] = sin[b * S5_CT:(b + 1) * S5_CT, tile(k)]
    a_re = [a16_ref[:, tile(k)] for k in range(nt)]
    a_im = [a16_ref[:, tile(nt + k)] for k in range(nt)]

    def step(c, carry):
        at_c = pl.ds(c, BATCH, stride=S5_PITCH)
        nxt_re, nxt_im = [], []
        for k in range(nt):
            xr, xi = carry[k], carry[nt + k]
            xp_scr[k, at_c, :] = xr
            xp_scr[nt + k, at_c, :] = xi
            nxt_re.append(a_re[k] * xr - a_im[k] * xi + sin_scr[k, at_c, :])
            nxt_im.append(a_re[k] * xi + a_im[k] * xr + sin_scr[nt + k, at_c, :])
        return tuple(nxt_re + nxt_im)

    state = lax.fori_loop(0, S5_CT, step, tuple(carry_scr[:, tile(k)] for k in range(2 * nt)),
                          unroll=4)
    for k in range(2 * nt):
        carry_scr[:, tile(k)] = state[k]

    xp = jnp.concatenate(
        [jnp.concatenate([xp_scr[k, batch_rows(b), :] for b in range(BATCH)], axis=0)
         for k in range(2 * nt)], axis=-1).astype(BF16)
    for j in range(S5_IN // S5_TB):
        cols = slice(j * S5_TB, (j + 1) * S5_TB)
        kend = (j + 1) * S5_TB
        y = (_dot(xp, wout_ref[:, cols]) + _dot(x_scr[:, :kend], wintra_ref[:kend, cols])
             + dsk_ref[:, cols] * x_scr[:, cols].astype(F32))
        g = jax.nn.gelu(y)
        for tt in range(S5_TB // LANES):
            t = j * (S5_TB // LANES) + tt
            y_ref[:, pl.ds(t, S5_CT, stride=S5_L), :] = (
                g[:, tt * LANES:(tt + 1) * LANES].reshape(BATCH, S5_CT, LANES))


def _s5(u, layer, wsin, wout, z, a16, dsk):
    rows = S5_CT * BATCH
    wspec = lambda a: pl.BlockSpec((None,) + a.shape[1:], lambda g, c: (layer * S5_NB + g, 0, 0))
    xspec = pl.BlockSpec((None, BATCH, S5_CT * S5_L, LANES), lambda g, c: (g, 0, c, 0))
    return pl.pallas_call(
        _s5_kernel,
        grid=(S5_NB, S5_CHUNKS // S5_CT),
        in_specs=[xspec, wspec(wsin), wspec(z), wspec(wout), wspec(a16), wspec(dsk)],
        out_specs=xspec,
        out_shape=jax.ShapeDtypeStruct(u.shape, F32),
        scratch_shapes=[pltpu.VMEM((rows, S5_IN), BF16),
                        pltpu.VMEM((2 * S5_ST // LANES, BATCH * S5_PITCH, LANES), F32),
                        pltpu.VMEM((2 * S5_ST // LANES, BATCH * S5_PITCH, LANES), F32),
                        pltpu.VMEM((BATCH, 2 * S5_ST), F32), pltpu.VMEM((S5_IN, S5_IN), BF16)],
        compiler_params=pltpu.CompilerParams(
            dimension_semantics=("arbitrary", "arbitrary"), vmem_limit_bytes=VMEM_LIMIT),
        name="s5",
    )(u, wsin, z, wout, a16, dsk)


def _mix_out_kernel(x_ref, o_ref, yg_ref, ga_ref, gs_ref, woa_ref, wglu_ref, wout_ref, gpost_ref,
                    xo_ref):
    y_att = _dot(o_ref[...], woa_ref[...])
    yg = jnp.concatenate([yg_ref[gb] for gb in range(S5_NB)], axis=-1).astype(BF16)
    y_ssm = _dot(yg, wglu_ref[:, :D_MODEL]) * jax.nn.sigmoid(_dot(yg, wglu_ref[:, D_MODEL:]))
    merged = ga_ref[...].astype(F32) * y_att + gs_ref[...].astype(F32) * y_ssm
    m = _dot(merged.astype(BF16), wout_ref[...])
    xo_ref[...] = x_ref[...] + _rms(m, gpost_ref[...])


def _mix_out(x, layer, o, yg, ga, gs, woa, wglu, wout, gpost, tm=512):
    row = lambda d: pl.BlockSpec((None, tm, d), lambda b, i: (b, i, 0))
    full = lambda a: _layer_spec(a, layer)
    yg_spec = pl.BlockSpec((S5_NB, None, tm, LANES), lambda b, i: (0, b, i, 0))
    return pl.pallas_call(
        _mix_out_kernel,
        grid=(BATCH, SEQ // tm),
        in_specs=[row(D_MODEL), row(N_HEADS * V_HEAD), yg_spec, row(D_MODEL), row(D_MODEL),
                  full(woa), full(wglu), full(wout), full(gpost)],
        out_specs=row(D_MODEL),
        out_shape=jax.ShapeDtypeStruct((BATCH, SEQ, D_MODEL), F32),
        compiler_params=pltpu.CompilerParams(
            dimension_semantics=("arbitrary", "arbitrary"), vmem_limit_bytes=VMEM_LIMIT),
        name="mix_out",
    )(x, o, yg, ga, gs, woa, wglu, wout, gpost)


FFN_HALO = 8


def _ffn_kernel(x_ref, gpre_ref, wup_ref, cw_ref, cb_ref, wdn_ref, gpost_ref, xo_ref, gate_scr):
    tm = x_ref.shape[0]
    i = pl.program_id(1)

    @pl.when(i == 0)
    def _():
        gate_scr[0:FFN_HALO, :] = jnp.zeros((FFN_HALO, D_FF), F32)

    @pl.when(i > 0)
    def _():
        gate_scr[0:FFN_HALO, :] = gate_scr[tm:tm + FFN_HALO, :]

    x = x_ref[...]
    h = _rms(x, gpre_ref[...]).astype(BF16)
    gate = _dot(h, wup_ref[:, :D_FF])
    val = _dot(h, wup_ref[:, D_FF:])
    gate_scr[FFN_HALO:, :] = gate
    conv = (cb_ref[...]
            + cw_ref[0:1, :] * gate_scr[pl.ds(FFN_HALO - 2, tm), :]
            + cw_ref[1:2, :] * gate_scr[pl.ds(FFN_HALO - 1, tm), :]
            + cw_ref[2:3, :] * gate)
    act = (jax.nn.gelu(conv) * val).astype(BF16)
    y = _dot(act, wdn_ref[...])
    xo_ref[...] = x + _rms(y, gpost_ref[...])


def _ffn(x, layer, gpre, wup, cw, cb, wdn, gpost, tm=512):
    row = pl.BlockSpec((None, tm, D_MODEL), lambda b, i: (b, i, 0))
    full = lambda a: _layer_spec(a, layer)
    return pl.pallas_call(
        _ffn_kernel,
        grid=(BATCH, SEQ // tm),
        in_specs=[row, full(gpre), full(wup), full(cw), full(cb), full(wdn), full(gpost)],
        out_specs=row,
        out_shape=jax.ShapeDtypeStruct((BATCH, SEQ, D_MODEL), F32),
        scratch_shapes=[pltpu.VMEM((tm + FFN_HALO, D_FF), F32)],
        compiler_params=pltpu.CompilerParams(
            dimension_semantics=("arbitrary", "arbitrary"), vmem_limit_bytes=VMEM_LIMIT),
        name="ffn",
    )(x, gpre, wup, cw, cb, wdn, gpost)


W_IN_ROWS = 256


def _w_in_layout_kernel(w_ref, o_ref):
    o1 = Q_LORA
    o2 = o1 + KV_LORA
    o3 = o2 + QK_ROPE
    o4 = o3 + SSM_WIDTH
    o_ref[:, C_Q0:C_KPE0] = w_ref[:, :o1].astype(BF16)
    o_ref[:, C_KPE0:C_KV0] = jnp.zeros((W_IN_ROWS, C_KV0 - C_KPE0), BF16)
    o_ref[:, C_KPE0:C_KPE0 + QK_ROPE] = w_ref[:, o2:o3].astype(BF16)
    o_ref[:, C_KV0:C_U0] = w_ref[:, o1:o2].astype(BF16)
    o_ref[:, C_U0:C_GA0] = w_ref[:, o3:o4].astype(BF16)
    o_ref[:, C_GA0:C_END] = w_ref[:, o4:].astype(BF16)


def _w_in_layout(w_in):
    cols = w_in.shape[-1]
    return pl.pallas_call(
        _w_in_layout_kernel,
        grid=(DEPTH, D_MODEL // W_IN_ROWS),
        in_specs=[pl.BlockSpec((None, W_IN_ROWS, cols), lambda l, r: (l, r, 0))],
        out_specs=pl.BlockSpec((None, W_IN_ROWS, C_END), lambda l, r: (l, r, 0)),
        out_shape=jax.ShapeDtypeStruct((DEPTH, D_MODEL, C_END), BF16),
        compiler_params=pltpu.CompilerParams(
            dimension_semantics=("arbitrary", "arbitrary"), vmem_limit_bytes=VMEM_LIMIT),
        name="w_in_layout",
    )(w_in)


def _layout_w_uq(w):
    w4 = w.reshape(DEPTH, Q_LORA, N_HEADS, QK_NOPE + QK_ROPE)
    pad = jnp.zeros((DEPTH, Q_LORA, N_HEADS, HEAD_PAD - QK_NOPE - QK_ROPE), w.dtype)
    return jnp.concatenate([w4, pad], axis=-1).reshape(DEPTH, Q_LORA, N_HEADS * HEAD_PAD).astype(BF16)


def _layout_w_ukv(w):
    w4 = w.reshape(DEPTH, KV_LORA, N_HEADS, QK_NOPE + V_HEAD)
    return (w4[..., :QK_NOPE].reshape(DEPTH, KV_LORA, N_HEADS * QK_NOPE).astype(BF16),
            w4[..., QK_NOPE:].reshape(DEPTH, KV_LORA, N_HEADS * V_HEAD).astype(BF16))


def _rope_tables():
    pos = jnp.arange(SEQ, dtype=F32)
    inv_freq = ROPE_THETA ** (-jnp.arange(0, QK_ROPE, 2, dtype=F32) / QK_ROPE)
    ang = pos[:, None] * inv_freq[None, :]
    cos, sin = jnp.cos(ang), jnp.sin(ang)
    scale = math.log2(math.e) / math.sqrt(QK_NOPE + QK_ROPE)
    ones = jnp.ones((SEQ, QK_NOPE), F32)
    zq = jnp.zeros((SEQ, HEAD_PAD - QK_NOPE - QK_ROPE), F32)
    cq = jnp.concatenate([ones, cos, cos, zq], axis=1) * scale
    sq = jnp.concatenate([0.0 * ones, -sin, sin, zq], axis=1) * scale
    zk = jnp.zeros((SEQ, LANES - QK_ROPE), F32)
    ck = jnp.concatenate([cos, cos, zk], axis=1)
    sk = jnp.concatenate([-sin, sin, zk], axis=1)
    return cq, sq, ck, sk


def kernel(x, w_in, b_gate, g_mix_pre, g_q, w_uq, g_kv, w_ukv, w_o_att, lam_re, lam_im, log_step, b_re, b_im, c_re, c_im, d_skip, w_glu, w_out, g_mix_post, g_ffn_pre, w_up, conv_w, conv_b, w_down, g_ffn_post):
    cq, sq, ck, sk = _rope_tables()
    wsin, wout_s5, z_s5, a16 = _s5_prep(lam_re, lam_im, log_step, b_re, b_im, c_re, c_im)
    dsk = jnp.tile(d_skip.reshape(DEPTH * S5_NB, 1, LANES), (1, 1, S5_L))
    cw = jnp.pad(conv_w, ((0, 0), (0, FFN_HALO - CONV_W), (0, 0)))
    vec = lambda a: a[:, None, :]
    win = _w_in_layout(w_in)
    wuq = _layout_w_uq(w_uq)
    wkn, wv = _layout_w_ukv(w_ukv)
    woa, wglu, wout, wup, wdn = (a.astype(BF16) for a in (w_o_att, w_glu, w_out, w_up, w_down))
    gpre, bg, gq, gkv, gpost = vec(g_mix_pre), vec(b_gate), vec(g_q), vec(g_kv), vec(g_mix_post)
    fpre, cb, fpost = vec(g_ffn_pre), vec(conv_b), vec(g_ffn_post)
    for l in range(DEPTH):
        q, k, v, u, ga, gs = _mix_in(x, l, gpre, win, bg, gq, wuq, gkv, wkn, wv, cq, sq, ck, sk)
        o = _attn(q, k, v)
        yg = _s5(u, l, wsin, wout_s5, z_s5, a16, dsk)
        x = _mix_out(x, l, o, yg, ga, gs, woa, wglu, wout, gpost)
        x = _ffn(x, l, fpre, wup, cw, cb, wdn, fpost)
    return x
```

```python
import math

import jax
import jax.numpy as jnp
from jax import lax
from jax.experimental import pallas as pl
from jax.experimental.pallas import tpu as pltpu

D_MODEL = 1024
BATCH = 16
SEQ = 2048
DEPTH = 4
CHUNK = 64
N_HEADS = 8
QK_NOPE = 64
QK_ROPE = 32
V_HEAD = 64
Q_LORA = 384
KV_LORA = 256
ROPE_THETA = 10000.0
SSM_WIDTH = 512
SSM_GROUP = 16
SSM_GROUPS = SSM_WIDTH // SSM_GROUP
SSM_STATE = 64
D_FF = 2816
CONV_W = 3
EPS = 1e-6

LANES = 128
HEAD_PAD = LANES
S5_L = 16
S5_CHUNKS = SEQ // S5_L
S5_GL = LANES // SSM_GROUP
S5_NB = SSM_GROUPS // S5_GL
S5_IN = S5_L * LANES
S5_ST = S5_GL * SSM_STATE
VMEM_LIMIT = 56 * 1024 * 1024

F32 = jnp.float32
BF16 = jnp.bfloat16


def _rms(x, g):
    var = jnp.mean(x * x, axis=-1, keepdims=True)
    return x * lax.rsqrt(var + EPS) * g


def _dot(a, b):
    return jnp.dot(a, b, preferred_element_type=F32)


def _dot_nt(a, b):
    return lax.dot_general(a, b, (((1,), (1,)), ((), ())), preferred_element_type=F32)


C_Q0, C_KPE0, C_KV0, C_U0, C_GA0, C_GS0, C_END = 0, 384, 512, 768, 1280, 2304, 3328
ROPE_HALF = QK_ROPE // 2


def _swap_halves(x, lo):
    lane = lax.broadcasted_iota(jnp.int32, x.shape, 1)
    first = (lane >= lo) & (lane < lo + ROPE_HALF)
    return jnp.where(first, pltpu.roll(x, LANES - ROPE_HALF, axis=1), pltpu.roll(x, ROPE_HALF, axis=1))


def _mix_in_kernel(x_ref, gpre_ref, win_ref, bg_ref, gq_ref, wuq_ref,
                   gkv_ref, wkn_ref, wv_ref, cq_ref, sq_ref, ck_ref, sk_ref,
                   q_ref, k_ref, v_ref, u_ref, ga_ref, gs_ref):
    h = _rms(x_ref[...], gpre_ref[...]).astype(BF16)

    z_q = _dot(h, win_ref[:, C_Q0:C_KV0])
    c_kv = _dot(h, win_ref[:, C_KV0:C_U0])

    u = _dot(h, win_ref[:, C_U0:C_GA0])
    for gb in range(S5_NB):
        u_ref[gb] = u[:, gb * LANES:(gb + 1) * LANES]

    cqn = _rms(z_q[:, :Q_LORA], gq_ref[...]).astype(BF16)
    ckvn = _rms(c_kv, gkv_ref[...]).astype(BF16)
    ga_ref[...] = jax.nn.sigmoid(
        _dot(h, win_ref[:, C_GA0:C_GS0]) + bg_ref[:, :D_MODEL]).astype(BF16)

    q = _dot(cqn, wuq_ref[...])
    kn = _dot(ckvn, wkn_ref[...])
    v_ref[...] = _dot(ckvn, wv_ref[...]).astype(BF16)
    gs_ref[...] = jax.nn.sigmoid(
        _dot(h, win_ref[:, C_GS0:C_END]) + bg_ref[:, D_MODEL:]).astype(BF16)

    cq, sq = cq_ref[...], sq_ref[...]
    for hd in range(N_HEADS):
        sl = slice(hd * HEAD_PAD, (hd + 1) * HEAD_PAD)
        qh = q[:, sl]
        q_ref[:, sl] = (qh * cq + _swap_halves(qh, QK_NOPE) * sq).astype(BF16)

    kpe = z_q[:, C_KPE0:C_KV0]
    kpe = pltpu.roll(kpe * ck_ref[...] + _swap_halves(kpe, 0) * sk_ref[...], QK_NOPE, axis=1)
    lane = lax.broadcasted_iota(jnp.int32, kpe.shape, 1)
    for hp in range(N_HEADS // 2):
        pair = kn[:, hp * LANES:(hp + 1) * LANES]
        for hh, nope in enumerate((pair, pltpu.roll(pair, QK_NOPE, axis=1))):
            sl = slice((2 * hp + hh) * HEAD_PAD, (2 * hp + hh + 1) * HEAD_PAD)
            k_ref[:, sl] = jnp.where(lane < QK_NOPE, nope, kpe).astype(BF16)


def _layer_spec(a, layer):
    return pl.BlockSpec((None,) + a.shape[1:], lambda *_: (layer,) + (0,) * (a.ndim - 1))


def _mix_in(x, layer, gpre, win, bg, gq, wuq, gkv, wkn, wv, cq, sq, ck, sk, tm=512):
    nt = SEQ // tm
    row = lambda d: pl.BlockSpec((None, tm, d), lambda i, b: (b, i, 0))
    full = lambda a: _layer_spec(a, layer)
    tab = pl.BlockSpec((tm, LANES), lambda i, b: (i, 0))
    hq = N_HEADS * HEAD_PAD
    outs = [(hq, BF16), (hq, BF16), (N_HEADS * V_HEAD, BF16), None, (D_MODEL, BF16), (D_MODEL, BF16)]
    u_spec = pl.BlockSpec((S5_NB, None, tm, LANES), lambda i, b: (0, b, i, 0))
    u_shape = jax.ShapeDtypeStruct((S5_NB, BATCH, SEQ, LANES), F32)
    return pl.pallas_call(
        _mix_in_kernel,
        grid=(nt, BATCH),
        in_specs=[row(D_MODEL), full(gpre), full(win), full(bg), full(gq), full(wuq),
                  full(gkv), full(wkn), full(wv), tab, tab, tab, tab],
        out_specs=[u_spec if o is None else row(o[0]) for o in outs],
        out_shape=[u_shape if o is None else jax.ShapeDtypeStruct((BATCH, SEQ, o[0]), o[1])
                   for o in outs],
        compiler_params=pltpu.CompilerParams(
            dimension_semantics=("arbitrary", "arbitrary"), vmem_limit_bytes=VMEM_LIMIT),
        name="mix_in",
    )(x, gpre, win, bg, gq, wuq, gkv, wkn, wv, cq, sq, ck, sk)


ATT_TQ = 256
ATT_SKEW = 1


def _attn_kernel(q_ref, k_ref, v_ref, o_ref):
    tq = ATT_TQ
    row_chunk = lax.broadcasted_iota(jnp.int32, (tq, tq), 0) // CHUNK
    col_chunk = lax.broadcasted_iota(jnp.int32, (tq, tq), 1) // CHUNK
    mask = row_chunk >= col_chunk
    lane = lax.broadcasted_iota(jnp.int32, (tq, 2 * V_HEAD), 1)

    def scores(i, hh):
        q0 = i * tq
        hs = slice(hh * HEAD_PAD, (hh + 1) * HEAD_PAD)
        qh = q_ref[q0:q0 + tq, hs]
        s_d = jnp.where(mask, _dot_nt(qh, k_ref[q0:q0 + tq, hs]), -jnp.inf)
        s_p = _dot_nt(qh, k_ref[0:q0, hs]) if i > 0 else None
        return s_d, s_p

    def softmax(s_d, s_p):
        m = jnp.max(s_d, axis=-1, keepdims=True)
        if s_p is not None:
            m = jnp.maximum(m, jnp.max(s_p, axis=-1, keepdims=True))
        p_d = jnp.exp2(s_d - m)
        l = jnp.sum(p_d, axis=-1, keepdims=True)
        p_p = None
        if s_p is not None:
            p_p = jnp.exp2(s_p - m)
            l = l + jnp.sum(p_p, axis=-1, keepdims=True)
            p_p = p_p.astype(BF16)
        return p_d.astype(BF16), p_p, l

    def weighted_values(i, p_d, p_p, l):
        q0 = i * tq
        acc = _dot(p_d, v_ref[q0:q0 + tq, :])
        if p_p is not None:
            acc = acc + _dot(p_p, v_ref[0:q0, :])
        return acc / l

    blocks = [(i, hh) for i in reversed(range(SEQ // tq)) for hh in range(2)]
    sc, sm, outs = {}, {}, {}
    for n in range(len(blocks) + 2 * ATT_SKEW):
        if n < len(blocks):
            sc[n] = scores(*blocks[n])
        if ATT_SKEW <= n < len(blocks) + ATT_SKEW:
            sm[n - ATT_SKEW] = softmax(*sc.pop(n - ATT_SKEW))
        if n >= 2 * ATT_SKEW:
            i, hh = blocks[n - 2 * ATT_SKEW]
            outs[hh] = weighted_values(i, *sm.pop(n - 2 * ATT_SKEW))
            if hh == 1:
                o_ref[i * tq:(i + 1) * tq, :] = jnp.where(lane < V_HEAD, outs[0], outs[1]).astype(BF16)


def _attn(q, k, v):
    blk = lambda d: pl.BlockSpec((None, SEQ, d), lambda b, p: (b, 0, p))
    return pl.pallas_call(
        _attn_kernel,
        grid=(BATCH, N_HEADS // 2),
        in_specs=[blk(2 * HEAD_PAD), blk(2 * HEAD_PAD), blk(2 * V_HEAD)],
        out_specs=blk(2 * V_HEAD),
        out_shape=jax.ShapeDtypeStruct((BATCH, SEQ, N_HEADS * V_HEAD), BF16),
        compiler_params=pltpu.CompilerParams(
            dimension_semantics=("arbitrary", "arbitrary"), vmem_limit_bytes=VMEM_LIMIT),
        name="attn",
    )(q, k, v)


def _s5_prep_kernel(lr_r, li_r, ls_r, lr_c, li_c, ls_c, b16r_ref, b16i_ref, cxr_ref, cxi_ref,
                    wsin_ref, wout_ref, z_ref, a16_ref):
    lr, li = lr_r[...], li_r[...]
    dt = jnp.exp(ls_r[...])
    mag = jnp.exp(lr * dt)
    abar_re = mag * jnp.cos(li * dt)
    abar_im = mag * jnp.sin(li * dt)
    nr, ni = abar_re - 1.0, abar_im
    den = lr * lr + li * li
    f_re = (nr * lr + ni * li) / den
    f_im = (ni * lr - nr * li) / den
    b16r, b16i = b16r_ref[...], b16i_ref[...]
    bbar_re = f_re * b16r - f_im * b16i
    bbar_im = f_re * b16i + f_im * b16r
    e_in = (S5_L - 1.0) - lax.broadcasted_iota(jnp.int32, (S5_L, 1), 0).astype(F32)
    pm = jnp.exp(e_in * (lr * dt))
    pr = pm * jnp.cos(e_in * (li * dt))
    pi = pm * jnp.sin(e_in * (li * dt))
    own = (lax.broadcasted_iota(jnp.int32, (LANES, S5_ST), 0) // SSM_GROUP
           == lax.broadcasted_iota(jnp.int32, (LANES, S5_ST), 1) // SSM_STATE)
    for s in range(S5_L):
        prs, pis = pr[s:s + 1, :], pi[s:s + 1, :]
        wre = jnp.concatenate([bbar_re * prs - bbar_im * pis] * S5_GL, axis=0)
        wim = jnp.concatenate([bbar_re * pis + bbar_im * prs] * S5_GL, axis=0)
        wre = jnp.where(own, wre, 0.0)
        wim = jnp.where(own, wim, 0.0)
        rows = slice(s * LANES, (s + 1) * LANES)
        wsin_ref[rows, :S5_ST] = wre.astype(BF16)
        wsin_ref[rows, S5_ST:] = wim.astype(BF16)
    a16_ref[:, :S5_ST] = jnp.exp(S5_L * lr * dt) * jnp.cos(S5_L * li * dt)
    a16_ref[:, S5_ST:] = jnp.exp(S5_L * lr * dt) * jnp.sin(S5_L * li * dt)

    lrc, lic = lr_c[...], li_c[...]
    dtc = jnp.exp(ls_c[...])
    e_out = lax.broadcasted_iota(jnp.int32, (1, S5_L), 1).astype(F32) + 1.0
    pm2 = jnp.exp(e_out * (lrc * dtc))
    ar2 = pm2 * jnp.cos(e_out * (lic * dtc))
    ai2 = pm2 * jnp.sin(e_out * (lic * dtc))
    own2 = (lax.broadcasted_iota(jnp.int32, (S5_ST, LANES), 0) // SSM_STATE
            == lax.broadcasted_iota(jnp.int32, (S5_ST, LANES), 1) // SSM_GROUP)
    cr = jnp.where(own2, cxr_ref[...], 0.0)
    ci = jnp.where(own2, cxi_ref[...], 0.0)
    for t in range(S5_L):
        art, ait = ar2[:, t:t + 1], ai2[:, t:t + 1]
        cols = slice(t * LANES, (t + 1) * LANES)
        wout_ref[:S5_ST, cols] = (cr * art - ci * ait).astype(BF16)
        wout_ref[S5_ST:, cols] = (-(cr * ait + ci * art)).astype(BF16)
    z = _dot(wsin_ref[...], jnp.concatenate([cr, -ci], axis=0).astype(BF16))
    z_ref[...] = z.astype(BF16)


def _s5_prep(lam_re, lam_im, log_step, b_re, b_im, c_re, c_im):
    n = DEPTH * S5_NB
    blk = lambda a: a.reshape((n, S5_GL) + a.shape[2:])
    ls = jnp.broadcast_to(log_step[..., None], (DEPTH, SSM_GROUPS, SSM_STATE))
    rows = [blk(a).reshape(n, 1, S5_ST) for a in (lam_re, lam_im, ls)]
    cols = [blk(a).reshape(n, S5_ST, 1) for a in (lam_re, lam_im, ls)]
    b16 = [jnp.transpose(blk(a), (0, 3, 1, 2)).reshape(n, SSM_GROUP, S5_ST) for a in (b_re, b_im)]
    cx = [jnp.tile(jnp.swapaxes(blk(a), 2, 3).reshape(n, S5_ST, SSM_GROUP), (1, 1, S5_GL))
          for a in (c_re, c_im)]
    spec = lambda *tail: pl.BlockSpec((None,) + tail, lambda i: (i,) + (0,) * len(tail))
    outs = [((S5_IN, 2 * S5_ST), BF16), ((2 * S5_ST, S5_IN), BF16), ((S5_IN, LANES), BF16),
            ((1, 2 * S5_ST), F32)]
    return pl.pallas_call(
        _s5_prep_kernel,
        grid=(n,),
        in_specs=([spec(1, S5_ST)] * 3 + [spec(S5_ST, 1)] * 3 + [spec(SSM_GROUP, S5_ST)] * 2
                  + [spec(S5_ST, LANES)] * 2),
        out_specs=[spec(*s) for s, _ in outs],
        out_shape=[jax.ShapeDtypeStruct((n,) + s, t) for s, t in outs],
        compiler_params=pltpu.CompilerParams(
            dimension_semantics=("arbitrary",), vmem_limit_bytes=VMEM_LIMIT),
        name="s5_prep",
    )(*rows, *cols, *b16, *cx)


S5_CT = 32
S5_PITCH = S5_CT + 8
S5_TB = 2 * LANES


def _s5_kernel(u_ref, wsin_ref, z_ref, wout_ref, a16_ref, dsk_ref, y_ref,
               x_scr, sin_scr, xp_scr, carry_scr, wintra_ref):
    rows = BATCH * S5_CT

    @pl.when(pl.program_id(1) == 0)
    def _():
        carry_scr[...] = jnp.zeros_like(carry_scr)
        for t in range(S5_L):
            keep = (t + 1) * LANES
            cols = slice(t * LANES, (t + 1) * LANES)
            wintra_ref[:keep, cols] = z_ref[S5_IN - keep:, :]
            if keep < S5_IN:
                wintra_ref[keep:, cols] = jnp.zeros((S5_IN - keep, LANES), BF16)

    for s in range(S5_L):
        xs = u_ref[:, pl.ds(s, S5_CT, stride=S5_L), :]
        x_scr[:, s * LANES:(s + 1) * LANES] = xs.reshape(rows, LANES).astype(BF16)

    nt = S5_ST // LANES
    tile = lambda k: slice(k * LANES, (k + 1) * LANES)
    batch_rows = lambda b: slice(b * S5_PITCH, b * S5_PITCH + S5_CT)
    sin = _dot(x_scr[...], wsin_ref[...])
    for k in range(2 * nt):
        for b in range(BATCH):
            sin_scr[k, batch_rows(b), :] = sin[b * S5_CT:(b + 1) * S5_CT, tile(k)]
    a_re = [a16_ref[:, tile(k)] for k in range(nt)]
    a_im = [a16_ref[:, tile(nt + k)] for k in range(nt)]

    def step(c, carry):
        at_c = pl.ds(c, BATCH, stride=S5_PITCH)
        nxt_re, nxt_im = [], []
        for k in range(nt):
            xr, xi = carry[k], carry[nt + k]
            xp_scr[k, at_c, :] = xr
            xp_scr[nt + k, at_c, :] = xi
            nxt_re.append(a_re[k] * xr - a_im[k] * xi + sin_scr[k, at_c, :])
            nxt_im.append(a_re[k] * xi + a_im[k] * xr + sin_scr[nt + k, at_c, :])
        return tuple(nxt_re + nxt_im)

    state = lax.fori_loop(0, S5_CT, step, tuple(carry_scr[:, tile(k)] for k in range(2 * nt)),
                          unroll=4)
    for k in range(2 * nt):
        carry_scr[:, tile(k)] = state[k]

    xp = jnp.concatenate(
        [jnp.concatenate([xp_scr[k, batch_rows(b), :] for b in range(BATCH)], axis=0)
         for k in range(2 * nt)], axis=-1).astype(BF16)
    for j in range(S5_IN // S5_TB):
        cols = slice(j * S5_TB, (j + 1) * S5_TB)
        kend = (j + 1) * S5_TB
        y = (_dot(xp, wout_ref[:, cols]) + _dot(x_scr[:, :kend], wintra_ref[:kend, cols])
             + dsk_ref[:, cols] * x_scr[:, cols].astype(F32))
        g = jax.nn.gelu(y)
        for tt in range(S5_TB // LANES):
            t = j * (S5_TB // LANES) + tt
            y_ref[:, pl.ds(t, S5_CT, stride=S5_L), :] = (
                g[:, tt * LANES:(tt + 1) * LANES].reshape(BATCH, S5_CT, LANES))


def _s5(u, layer, wsin, wout, z, a16, dsk):
    rows = S5_CT * BATCH
    wspec = lambda a: pl.BlockSpec((None,) + a.shape[1:], lambda g, c: (layer * S5_NB + g, 0, 0))
    xspec = pl.BlockSpec((None, BATCH, S5_CT * S5_L, LANES), lambda g, c: (g, 0, c, 0))
    return pl.pallas_call(
        _s5_kernel,
        grid=(S5_NB, S5_CHUNKS // S5_CT),
        in_specs=[xspec, wspec(wsin), wspec(z), wspec(wout), wspec(a16), wspec(dsk)],
        out_specs=xspec,
        out_shape=jax.ShapeDtypeStruct(u.shape, F32),
        scratch_shapes=[pltpu.VMEM((rows, S5_IN), BF16),
                        pltpu.VMEM((2 * S5_ST // LANES, BATCH * S5_PITCH, LANES), F32),
                        pltpu.VMEM((2 * S5_ST // LANES, BATCH * S5_PITCH, LANES), F32),
                        pltpu.VMEM((BATCH, 2 * S5_ST), F32), pltpu.VMEM((S5_IN, S5_IN), BF16)],
        compiler_params=pltpu.CompilerParams(
            dimension_semantics=("arbitrary", "arbitrary"), vmem_limit_bytes=VMEM_LIMIT),
        name="s5",
    )(u, wsin, z, wout, a16, dsk)


def _mix_out_kernel(x_ref, o_ref, yg_ref, ga_ref, gs_ref, woa_ref, wglu_ref, wout_ref, gpost_ref,
                    xo_ref):
    half = x_ref.shape[0] // 2

    def branches(r):
        y_att = _dot(o_ref[r, :], woa_ref[...])
        yg = jnp.concatenate([yg_ref[gb, r, :] for gb in range(S5_NB)], axis=-1).astype(BF16)
        y_ssm = _dot(yg, wglu_ref[:, :D_MODEL]) * jax.nn.sigmoid(_dot(yg, wglu_ref[:, D_MODEL:]))
        return ga_ref[r, :].astype(F32) * y_att + gs_ref[r, :].astype(F32) * y_ssm

    def project(r, merged):
        m = _dot(merged.astype(BF16), wout_ref[...])
        xo_ref[r, :] = x_ref[r, :] + _rms(m, gpost_ref[...])

    r0, r1 = slice(0, half), slice(half, 2 * half)
    merged0 = branches(r0)
    merged1 = branches(r1)
    project(r0, merged0)
    project(r1, merged1)


def _mix_out(x, layer, o, yg, ga, gs, woa, wglu, wout, gpost, tm=512):
    row = lambda d: pl.BlockSpec((None, tm, d), lambda b, i: (b, i, 0))
    full = lambda a: _layer_spec(a, layer)
    yg_spec = pl.BlockSpec((S5_NB, None, tm, LANES), lambda b, i: (0, b, i, 0))
    return pl.pallas_call(
        _mix_out_kernel,
        grid=(BATCH, SEQ // tm),
        in_specs=[row(D_MODEL), row(N_HEADS * V_HEAD), yg_spec, row(D_MODEL), row(D_MODEL),
                  full(woa), full(wglu), full(wout), full(gpost)],
        out_specs=row(D_MODEL),
        out_shape=jax.ShapeDtypeStruct((BATCH, SEQ, D_MODEL), F32),
        compiler_params=pltpu.CompilerParams(
            dimension_semantics=("arbitrary", "arbitrary"), vmem_limit_bytes=VMEM_LIMIT),
        name="mix_out",
    )(x, o, yg, ga, gs, woa, wglu, wout, gpost)


FFN_HALO = 8


def _ffn_kernel(x_ref, gpre_ref, wup_ref, cw_ref, cb_ref, wdn_ref, gpost_ref, xo_ref, gate_scr):
    tm = x_ref.shape[0]
    i = pl.program_id(1)

    @pl.when(i == 0)
    def _():
        gate_scr[0:FFN_HALO, :] = jnp.zeros((FFN_HALO, D_FF), F32)

    @pl.when(i > 0)
    def _():
        gate_scr[0:FFN_HALO, :] = gate_scr[tm:tm + FFN_HALO, :]

    x = x_ref[...]
    h = _rms(x, gpre_ref[...]).astype(BF16)
    gate = _dot(h, wup_ref[:, :D_FF])
    val = _dot(h, wup_ref[:, D_FF:])
    gate_scr[FFN_HALO:, :] = gate
    conv = (cb_ref[...]
            + cw_ref[0:1, :] * gate_scr[pl.ds(FFN_HALO - 2, tm), :]
            + cw_ref[1:2, :] * gate_scr[pl.ds(FFN_HALO - 1, tm), :]
            + cw_ref[2:3, :] * gate)
    act = (jax.nn.gelu(conv) * val).astype(BF16)
    y = _dot(act, wdn_ref[...])
    xo_ref[...] = x + _rms(y, gpost_ref[...])


def _ffn(x, layer, gpre, wup, cw, cb, wdn, gpost, tm=512):
    row = pl.BlockSpec((None, tm, D_MODEL), lambda b, i: (b, i, 0))
    full = lambda a: _layer_spec(a, layer)
    return pl.pallas_call(
        _ffn_kernel,
        grid=(BATCH, SEQ // tm),
        in_specs=[row, full(gpre), full(wup), full(cw), full(cb), full(wdn), full(gpost)],
        out_specs=row,
        out_shape=jax.ShapeDtypeStruct((BATCH, SEQ, D_MODEL), F32),
        scratch_shapes=[pltpu.VMEM((tm + FFN_HALO, D_FF), F32)],
        compiler_params=pltpu.CompilerParams(
            dimension_semantics=("arbitrary", "arbitrary"), vmem_limit_bytes=VMEM_LIMIT),
        name="ffn",
    )(x, gpre, wup, cw, cb, wdn, gpost)


W_IN_ROWS = 256


def _w_in_layout_kernel(w_ref, o_ref):
    o1 = Q_LORA
    o2 = o1 + KV_LORA
    o3 = o2 + QK_ROPE
    o4 = o3 + SSM_WIDTH
    o_ref[:, C_Q0:C_KPE0] = w_ref[:, :o1].astype(BF16)
    o_ref[:, C_KPE0:C_KV0] = jnp.zeros((W_IN_ROWS, C_KV0 - C_KPE0), BF16)
    o_ref[:, C_KPE0:C_KPE0 + QK_ROPE] = w_ref[:, o2:o3].astype(BF16)
    o_ref[:, C_KV0:C_U0] = w_ref[:, o1:o2].astype(BF16)
    o_ref[:, C_U0:C_GA0] = w_ref[:, o3:o4].astype(BF16)
    o_ref[:, C_GA0:C_END] = w_ref[:, o4:].astype(BF16)


def _w_in_layout(w_in):
    cols = w_in.shape[-1]
    return pl.pallas_call(
        _w_in_layout_kernel,
        grid=(DEPTH, D_MODEL // W_IN_ROWS),
        in_specs=[pl.BlockSpec((None, W_IN_ROWS, cols), lambda l, r: (l, r, 0))],
        out_specs=pl.BlockSpec((None, W_IN_ROWS, C_END), lambda l, r: (l, r, 0)),
        out_shape=jax.ShapeDtypeStruct((DEPTH, D_MODEL, C_END), BF16),
        compiler_params=pltpu.CompilerParams(
            dimension_semantics=("arbitrary", "arbitrary"), vmem_limit_bytes=VMEM_LIMIT),
        name="w_in_layout",
    )(w_in)


def _layout_w_uq(w):
    w4 = w.reshape(DEPTH, Q_LORA, N_HEADS, QK_NOPE + QK_ROPE)
    pad = jnp.zeros((DEPTH, Q_LORA, N_HEADS, HEAD_PAD - QK_NOPE - QK_ROPE), w.dtype)
    return jnp.concatenate([w4, pad], axis=-1).reshape(DEPTH, Q_LORA, N_HEADS * HEAD_PAD).astype(BF16)


def _layout_w_ukv(w):
    w4 = w.reshape(DEPTH, KV_LORA, N_HEADS, QK_NOPE + V_HEAD)
    return (w4[..., :QK_NOPE].reshape(DEPTH, KV_LORA, N_HEADS * QK_NOPE).astype(BF16),
            w4[..., QK_NOPE:].reshape(DEPTH, KV_LORA, N_HEADS * V_HEAD).astype(BF16))


def _rope_tables():
    pos = jnp.arange(SEQ, dtype=F32)
    inv_freq = ROPE_THETA ** (-jnp.arange(0, QK_ROPE, 2, dtype=F32) / QK_ROPE)
    ang = pos[:, None] * inv_freq[None, :]
    cos, sin = jnp.cos(ang), jnp.sin(ang)
    scale = math.log2(math.e) / math.sqrt(QK_NOPE + QK_ROPE)
    ones = jnp.ones((SEQ, QK_NOPE), F32)
    zq = jnp.zeros((SEQ, HEAD_PAD - QK_NOPE - QK_ROPE), F32)
    cq = jnp.concatenate([ones, cos, cos, zq], axis=1) * scale
    sq = jnp.concatenate([0.0 * ones, -sin, sin, zq], axis=1) * scale
    zk = jnp.zeros((SEQ, LANES - QK_ROPE), F32)
    ck = jnp.concatenate([cos, cos, zk], axis=1)
    sk = jnp.concatenate([-sin, sin, zk], axis=1)
    return cq, sq, ck, sk


def kernel(x, w_in, b_gate, g_mix_pre, g_q, w_uq, g_kv, w_ukv, w_o_att, lam_re, lam_im, log_step, b_re, b_im, c_re, c_im, d_skip, w_glu, w_out, g_mix_post, g_ffn_pre, w_up, conv_w, conv_b, w_down, g_ffn_post):
    cq, sq, ck, sk = _rope_tables()
    wsin, wout_s5, z_s5, a16 = _s5_prep(lam_re, lam_im, log_step, b_re, b_im, c_re, c_im)
    dsk = jnp.tile(d_skip.reshape(DEPTH * S5_NB, 1, LANES), (1, 1, S5_L))
    cw = jnp.pad(conv_w, ((0, 0), (0, FFN_HALO - CONV_W), (0, 0)))
    vec = lambda a: a[:, None, :]
    win = _w_in_layout(w_in)
    wuq = _layout_w_uq(w_uq)
    wkn, wv = _layout_w_ukv(w_ukv)
    woa, wglu, wout, wup, wdn = (a.astype(BF16) for a in (w_o_att, w_glu, w_out, w_up, w_down))
    gpre, bg, gq, gkv, gpost = vec(g_mix_pre), vec(b_gate), vec(g_q), vec(g_kv), vec(g_mix_post)
    fpre, cb, fpost = vec(g_ffn_pre), vec(conv_b), vec(g_ffn_post)
    for l in range(DEPTH):
        q, k, v, u, ga, gs = _mix_in(x, l, gpre, win, bg, gq, wuq, gkv, wkn, wv, cq, sq, ck, sk)
        o = _attn(q, k, v)
        yg = _s5(u, l, wsin, wout_s5, z_s5, a16, dsk)
        x = _mix_out(x, l, o, yg, ga, gs, woa, wglu, wout, gpost)
        x = _ffn(x, l, fpre, wup, cw, cb, wdn, fpost)
    return x
```

```python
import math

import jax
import jax.numpy as jnp
from jax import lax
from jax.experimental import pallas as pl
from jax.experimental.pallas import tpu as pltpu

D_MODEL = 1024
BATCH = 16
SEQ = 2048
DEPTH = 4
CHUNK = 64
N_HEADS = 8
QK_NOPE = 64
QK_ROPE = 32
V_HEAD = 64
Q_LORA = 384
KV_LORA = 256
ROPE_THETA = 10000.0
SSM_WIDTH = 512
SSM_GROUP = 16
SSM_GROUPS = SSM_WIDTH // SSM_GROUP
SSM_STATE = 64
D_FF = 2816
CONV_W = 3
EPS = 1e-6

LANES = 128
HEAD_PAD = LANES
S5_L = 16
S5_CHUNKS = SEQ // S5_L
S5_GL = LANES // SSM_GROUP
S5_NB = SSM_GROUPS // S5_GL
S5_IN = S5_L * LANES
S5_ST = S5_GL * SSM_STATE
VMEM_LIMIT = 56 * 1024 * 1024

F32 = jnp.float32
BF16 = jnp.bfloat16


def _rms(x, g):
    var = jnp.mean(x * x, axis=-1, keepdims=True)
    return x * lax.rsqrt(var + EPS) * g


def _dot(a, b):
    return jnp.dot(a, b, preferred_element_type=F32)


def _dot_nt(a, b):
    return lax.dot_general(a, b, (((1,), (1,)), ((), ())), preferred_element_type=F32)


C_Q0, C_KPE0, C_KV0, C_U0, C_GA0, C_GS0, C_END = 0, 384, 512, 768, 1280, 2304, 3328
ROPE_HALF = QK_ROPE // 2


def _swap_halves(x, lo):
    lane = lax.broadcasted_iota(jnp.int32, x.shape, 1)
    first = (lane >= lo) & (lane < lo + ROPE_HALF)
    return jnp.where(first, pltpu.roll(x, LANES - ROPE_HALF, axis=1), pltpu.roll(x, ROPE_HALF, axis=1))


def _mix_in_kernel(x_ref, gpre_ref, win_ref, bg_ref, gq_ref, wuq_ref,
                   gkv_ref, wkn_ref, wv_ref, cq_ref, sq_ref, ck_ref, sk_ref,
                   q_ref, k_ref, v_ref, u_ref, ga_ref, gs_ref):
    half = x_ref.shape[0] // 2

    def latents(r):
        h = _rms(x_ref[r, :], gpre_ref[...]).astype(BF16)
        z_q = _dot(h, win_ref[:, C_Q0:C_KV0])
        c_kv = _dot(h, win_ref[:, C_KV0:C_U0])
        u = _dot(h, win_ref[:, C_U0:C_GA0])
        for gb in range(S5_NB):
            u_ref[gb, r, :] = u[:, gb * LANES:(gb + 1) * LANES]
        return h, z_q, c_kv

    def up_project(r, h, z_q, c_kv):
        cqn = _rms(z_q[:, :Q_LORA], gq_ref[...]).astype(BF16)
        ckvn = _rms(c_kv, gkv_ref[...]).astype(BF16)
        ga_ref[r, :] = jax.nn.sigmoid(
            _dot(h, win_ref[:, C_GA0:C_GS0]) + bg_ref[:, :D_MODEL]).astype(BF16)
        q = _dot(cqn, wuq_ref[...])
        kn = _dot(ckvn, wkn_ref[...])
        v_ref[r, :] = _dot(ckvn, wv_ref[...]).astype(BF16)
        gs_ref[r, :] = jax.nn.sigmoid(
            _dot(h, win_ref[:, C_GS0:C_END]) + bg_ref[:, D_MODEL:]).astype(BF16)
        return q, kn

    def rotate(r, z_q, q, kn):
        cq, sq = cq_ref[r, :], sq_ref[r, :]
        for hd in range(N_HEADS):
            sl = slice(hd * HEAD_PAD, (hd + 1) * HEAD_PAD)
            qh = q[:, sl]
            q_ref[r, sl] = (qh * cq + _swap_halves(qh, QK_NOPE) * sq).astype(BF16)
        kpe = z_q[:, C_KPE0:C_KV0]
        kpe = pltpu.roll(kpe * ck_ref[r, :] + _swap_halves(kpe, 0) * sk_ref[r, :], QK_NOPE, axis=1)
        lane = lax.broadcasted_iota(jnp.int32, kpe.shape, 1)
        for hp in range(N_HEADS // 2):
            pair = kn[:, hp * LANES:(hp + 1) * LANES]
            for hh, nope in enumerate((pair, pltpu.roll(pair, QK_NOPE, axis=1))):
                sl = slice((2 * hp + hh) * HEAD_PAD, (2 * hp + hh + 1) * HEAD_PAD)
                k_ref[r, sl] = jnp.where(lane < QK_NOPE, nope, kpe).astype(BF16)

    r0, r1 = slice(0, half), slice(half, 2 * half)
    lat0 = latents(r0)
    lat1 = latents(r1)
    up0 = up_project(r0, *lat0)
    up1 = up_project(r1, *lat1)
    rotate(r0, lat0[1], *up0)
    rotate(r1, lat1[1], *up1)


def _layer_spec(a, layer):
    return pl.BlockSpec((None,) + a.shape[1:], lambda *_: (layer,) + (0,) * (a.ndim - 1))


def _mix_in(x, layer, gpre, win, bg, gq, wuq, gkv, wkn, wv, cq, sq, ck, sk, tm=512):
    nt = SEQ // tm
    row = lambda d: pl.BlockSpec((None, tm, d), lambda i, b: (b, i, 0))
    full = lambda a: _layer_spec(a, layer)
    tab = pl.BlockSpec((tm, LANES), lambda i, b: (i, 0))
    hq = N_HEADS * HEAD_PAD
    outs = [(hq, BF16), (hq, BF16), (N_HEADS * V_HEAD, BF16), None, (D_MODEL, BF16), (D_MODEL, BF16)]
    u_spec = pl.BlockSpec((S5_NB, None, tm, LANES), lambda i, b: (0, b, i, 0))
    u_shape = jax.ShapeDtypeStruct((S5_NB, BATCH, SEQ, LANES), F32)
    return pl.pallas_call(
        _mix_in_kernel,
        grid=(nt, BATCH),
        in_specs=[row(D_MODEL), full(gpre), full(win), full(bg), full(gq), full(wuq),
                  full(gkv), full(wkn), full(wv), tab, tab, tab, tab],
        out_specs=[u_spec if o is None else row(o[0]) for o in outs],
        out_shape=[u_shape if o is None else jax.ShapeDtypeStruct((BATCH, SEQ, o[0]), o[1])
                   for o in outs],
        compiler_params=pltpu.CompilerParams(
            dimension_semantics=("arbitrary", "arbitrary"), vmem_limit_bytes=VMEM_LIMIT),
        name="mix_in",
    )(x, gpre, win, bg, gq, wuq, gkv, wkn, wv, cq, sq, ck, sk)


ATT_TQ = 256
ATT_SKEW = 1


def _attn_kernel(q_ref, k_ref, v_ref, o_ref):
    tq = ATT_TQ
    row_chunk = lax.broadcasted_iota(jnp.int32, (tq, tq), 0) // CHUNK
    col_chunk = lax.broadcasted_iota(jnp.int32, (tq, tq), 1) // CHUNK
    mask = row_chunk >= col_chunk
    lane = lax.broadcasted_iota(jnp.int32, (tq, 2 * V_HEAD), 1)

    def scores(i, hh):
        q0 = i * tq
        hs = slice(hh * HEAD_PAD, (hh + 1) * HEAD_PAD)
        qh = q_ref[q0:q0 + tq, hs]
        s_d = jnp.where(mask, _dot_nt(qh, k_ref[q0:q0 + tq, hs]), -jnp.inf)
        s_p = _dot_nt(qh, k_ref[0:q0, hs]) if i > 0 else None
        return s_d, s_p

    def softmax(s_d, s_p):
        m = jnp.max(s_d, axis=-1, keepdims=True)
        if s_p is not None:
            m = jnp.maximum(m, jnp.max(s_p, axis=-1, keepdims=True))
        p_d = jnp.exp2(s_d - m)
        l = jnp.sum(p_d, axis=-1, keepdims=True)
        p_p = None
        if s_p is not None:
            p_p = jnp.exp2(s_p - m)
            l = l + jnp.sum(p_p, axis=-1, keepdims=True)
            p_p = p_p.astype(BF16)
        return p_d.astype(BF16), p_p, l

    def weighted_values(i, p_d, p_p, l):
        q0 = i * tq
        acc = _dot(p_d, v_ref[q0:q0 + tq, :])
        if p_p is not None:
            acc = acc + _dot(p_p, v_ref[0:q0, :])
        return acc / l

    blocks = [(i, hh) for i in reversed(range(SEQ // tq)) for hh in range(2)]
    sc, sm, outs = {}, {}, {}
    for n in range(len(blocks) + 2 * ATT_SKEW):
        if n < len(blocks):
            sc[n] = scores(*blocks[n])
        if ATT_SKEW <= n < len(blocks) + ATT_SKEW:
            sm[n - ATT_SKEW] = softmax(*sc.pop(n - ATT_SKEW))
        if n >= 2 * ATT_SKEW:
            i, hh = blocks[n - 2 * ATT_SKEW]
            outs[hh] = weighted_values(i, *sm.pop(n - 2 * ATT_SKEW))
            if hh == 1:
                o_ref[i * tq:(i + 1) * tq, :] = jnp.where(lane < V_HEAD, outs[0], outs[1]).astype(BF16)


def _attn(q, k, v):
    blk = lambda d: pl.BlockSpec((None, SEQ, d), lambda b, p: (b, 0, p))
    return pl.pallas_call(
        _attn_kernel,
        grid=(BATCH, N_HEADS // 2),
        in_specs=[blk(2 * HEAD_PAD), blk(2 * HEAD_PAD), blk(2 * V_HEAD)],
        out_specs=blk(2 * V_HEAD),
        out_shape=jax.ShapeDtypeStruct((BATCH, SEQ, N_HEADS * V_HEAD), BF16),
        compiler_params=pltpu.CompilerParams(
            dimension_semantics=("arbitrary", "arbitrary"), vmem_limit_bytes=VMEM_LIMIT),
        name="attn",
    )(q, k, v)


def _s5_prep_kernel(lr_r, li_r, ls_r, lr_c, li_c, ls_c, b16r_ref, b16i_ref, cxr_ref, cxi_ref,
                    wsin_ref, wout_ref, z_ref, a16_ref):
    lr, li = lr_r[...], li_r[...]
    dt = jnp.exp(ls_r[...])
    mag = jnp.exp(lr * dt)
    abar_re = mag * jnp.cos(li * dt)
    abar_im = mag * jnp.sin(li * dt)
    nr, ni = abar_re - 1.0, abar_im
    den = lr * lr + li * li
    f_re = (nr * lr + ni * li) / den
    f_im = (ni * lr - nr * li) / den
    b16r, b16i = b16r_ref[...], b16i_ref[...]
    bbar_re = f_re * b16r - f_im * b16i
    bbar_im = f_re * b16i + f_im * b16r
    e_in = (S5_L - 1.0) - lax.broadcasted_iota(jnp.int32, (S5_L, 1), 0).astype(F32)
    pm = jnp.exp(e_in * (lr * dt))
    pr = pm * jnp.cos(e_in * (li * dt))
    pi = pm * jnp.sin(e_in * (li * dt))
    own = (lax.broadcasted_iota(jnp.int32, (LANES, S5_ST), 0) // SSM_GROUP
           == lax.broadcasted_iota(jnp.int32, (LANES, S5_ST), 1) // SSM_STATE)
    for s in range(S5_L):
        prs, pis = pr[s:s + 1, :], pi[s:s + 1, :]
        wre = jnp.concatenate([bbar_re * prs - bbar_im * pis] * S5_GL, axis=0)
        wim = jnp.concatenate([bbar_re * pis + bbar_im * prs] * S5_GL, axis=0)
        wre = jnp.where(own, wre, 0.0)
        wim = jnp.where(own, wim, 0.0)
        rows = slice(s * LANES, (s + 1) * LANES)
        wsin_ref[rows, :S5_ST] = wre.astype(BF16)
        wsin_ref[rows, S5_ST:] = wim.astype(BF16)
    a16_ref[:, :S5_ST] = jnp.exp(S5_L * lr * dt) * jnp.cos(S5_L * li * dt)
    a16_ref[:, S5_ST:] = jnp.exp(S5_L * lr * dt) * jnp.sin(S5_L * li * dt)

    lrc, lic = lr_c[...], li_c[...]
    dtc = jnp.exp(ls_c[...])
    e_out = lax.broadcasted_iota(jnp.int32, (1, S5_L), 1).astype(F32) + 1.0
    pm2 = jnp.exp(e_out * (lrc * dtc))
    ar2 = pm2 * jnp.cos(e_out * (lic * dtc))
    ai2 = pm2 * jnp.sin(e_out * (lic * dtc))
    own2 = (lax.broadcasted_iota(jnp.int32, (S5_ST, LANES), 0) // SSM_STATE
            == lax.broadcasted_iota(jnp.int32, (S5_ST, LANES), 1) // SSM_GROUP)
    cr = jnp.where(own2, cxr_ref[...], 0.0)
    ci = jnp.where(own2, cxi_ref[...], 0.0)
    for t in range(S5_L):
        art, ait = ar2[:, t:t + 1], ai2[:, t:t + 1]
        cols = slice(t * LANES, (t + 1) * LANES)
        wout_ref[:S5_ST, cols] = (cr * art - ci * ait).astype(BF16)
        wout_ref[S5_ST:, cols] = (-(cr * ait + ci * art)).astype(BF16)
    z = _dot(wsin_ref[...], jnp.concatenate([cr, -ci], axis=0).astype(BF16))
    z_ref[...] = z.astype(BF16)


def _s5_prep(lam_re, lam_im, log_step, b_re, b_im, c_re, c_im):
    n = DEPTH * S5_NB
    blk = lambda a: a.reshape((n, S5_GL) + a.shape[2:])
    ls = jnp.broadcast_to(log_step[..., None], (DEPTH, SSM_GROUPS, SSM_STATE))
    rows = [blk(a).reshape(n, 1, S5_ST) for a in (lam_re, lam_im, ls)]
    cols = [blk(a).reshape(n, S5_ST, 1) for a in (lam_re, lam_im, ls)]
    b16 = [jnp.transpose(blk(a), (0, 3, 1, 2)).reshape(n, SSM_GROUP, S5_ST) for a in (b_re, b_im)]
    cx = [jnp.tile(jnp.swapaxes(blk(a), 2, 3).reshape(n, S5_ST, SSM_GROUP), (1, 1, S5_GL))
          for a in (c_re, c_im)]
    spec = lambda *tail: pl.BlockSpec((None,) + tail, lambda i: (i,) + (0,) * len(tail))
    outs = [((S5_IN, 2 * S5_ST), BF16), ((2 * S5_ST, S5_IN), BF16), ((S5_IN, LANES), BF16),
            ((1, 2 * S5_ST), F32)]
    return pl.pallas_call(
        _s5_prep_kernel,
        grid=(n,),
        in_specs=([spec(1, S5_ST)] * 3 + [spec(S5_ST, 1)] * 3 + [spec(SSM_GROUP, S5_ST)] * 2
                  + [spec(S5_ST, LANES)] * 2),
        out_specs=[spec(*s) for s, _ in outs],
        out_shape=[jax.ShapeDtypeStruct((n,) + s, t) for s, t in outs],
        compiler_params=pltpu.CompilerParams(
            dimension_semantics=("arbitrary",), vmem_limit_bytes=VMEM_LIMIT),
        name="s5_prep",
    )(*rows, *cols, *b16, *cx)


S5_CT = 32
S5_PITCH = S5_CT + 8
S5_TB = 2 * LANES


def _s5_kernel(u_ref, wsin_ref, z_ref, wout_ref, a16_ref, dsk_ref, y_ref,
               x_scr, sin_scr, xp_scr, carry_scr, wintra_ref):
    rows = BATCH * S5_CT

    @pl.when(pl.program_id(1) == 0)
    def _():
        carry_scr[...] = jnp.zeros_like(carry_scr)
        for t in range(S5_L):
            keep = (t + 1) * LANES
            cols = slice(t * LANES, (t + 1) * LANES)
            wintra_ref[:keep, cols] = z_ref[S5_IN - keep:, :]
            if keep < S5_IN:
                wintra_ref[keep:, cols] = jnp.zeros((S5_IN - keep, LANES), BF16)

    for s in range(S5_L):
        xs = u_ref[:, pl.ds(s, S5_CT, stride=S5_L), :]
        x_scr[:, s * LANES:(s + 1) * LANES] = xs.reshape(rows, LANES).astype(BF16)

    nt = S5_ST // LANES
    tile = lambda k: slice(k * LANES, (k + 1) * LANES)
    batch_rows = lambda b: slice(b * S5_PITCH, b * S5_PITCH + S5_CT)
    sin = _dot(x_scr[...], wsin_ref[...])
    for k in range(2 * nt):
        for b in range(BATCH):
            sin_scr[k, batch_rows(b), :] = sin[b * S5_CT:(b + 1) * S5_CT, tile(k)]
    a_re = [a16_ref[:, tile(k)] for k in range(nt)]
    a_im = [a16_ref[:, tile(nt + k)] for k in range(nt)]

    def step(c, carry):
        at_c = pl.ds(c, BATCH, stride=S5_PITCH)
        nxt_re, nxt_im = [], []
        for k in range(nt):
            xr, xi = carry[k], carry[nt + k]
            xp_scr[k, at_c, :] = xr
            xp_scr[nt + k, at_c, :] = xi
            nxt_re.append(a_re[k] * xr - a_im[k] * xi + sin_scr[k, at_c, :])
            nxt_im.append(a_re[k] * xi + a_im[k] * xr + sin_scr[nt + k, at_c, :])
        return tuple(nxt_re + nxt_im)

    state = lax.fori_loop(0, S5_CT, step, tuple(carry_scr[:, tile(k)] for k in range(2 * nt)),
                          unroll=4)
    for k in range(2 * nt):
        carry_scr[:, tile(k)] = state[k]

    xp = jnp.concatenate(
        [jnp.concatenate([xp_scr[k, batch_rows(b), :] for b in range(BATCH)], axis=0)
         for k in range(2 * nt)], axis=-1).astype(BF16)
    for j in range(S5_IN // S5_TB):
        cols = slice(j * S5_TB, (j + 1) * S5_TB)
        kend = (j + 1) * S5_TB
        y = (_dot(xp, wout_ref[:, cols]) + _dot(x_scr[:, :kend], wintra_ref[:kend, cols])
             + dsk_ref[:, cols] * x_scr[:, cols].astype(F32))
        g = jax.nn.gelu(y)
        for tt in range(S5_TB // LANES):
            t = j * (S5_TB // LANES) + tt
            y_ref[:, pl.ds(t, S5_CT, stride=S5_L), :] = (
                g[:, tt * LANES:(tt + 1) * LANES].reshape(BATCH, S5_CT, LANES))


def _s5(u, layer, wsin, wout, z, a16, dsk):
    rows = S5_CT * BATCH
    wspec = lambda a: pl.BlockSpec((None,) + a.shape[1:], lambda g, c: (layer * S5_NB + g, 0, 0))
    xspec = pl.BlockSpec((None, BATCH, S5_CT * S5_L, LANES), lambda g, c: (g, 0, c, 0))
    return pl.pallas_call(
        _s5_kernel,
        grid=(S5_NB, S5_CHUNKS // S5_CT),
        in_specs=[xspec, wspec(wsin), wspec(z), wspec(wout), wspec(a16), wspec(dsk)],
        out_specs=xspec,
        out_shape=jax.ShapeDtypeStruct(u.shape, F32),
        scratch_shapes=[pltpu.VMEM((rows, S5_IN), BF16),
                        pltpu.VMEM((2 * S5_ST // LANES, BATCH * S5_PITCH, LANES), F32),
                        pltpu.VMEM((2 * S5_ST // LANES, BATCH * S5_PITCH, LANES), F32),
                        pltpu.VMEM((BATCH, 2 * S5_ST), F32), pltpu.VMEM((S5_IN, S5_IN), BF16)],
        compiler_params=pltpu.CompilerParams(
            dimension_semantics=("arbitrary", "arbitrary"), vmem_limit_bytes=VMEM_LIMIT),
        name="s5",
    )(u, wsin, z, wout, a16, dsk)


def _mix_out_kernel(x_ref, o_ref, yg_ref, ga_ref, gs_ref, woa_ref, wglu_ref, wout_ref, gpost_ref,
                    xo_ref):
    half = x_ref.shape[0] // 2

    def branches(r):
        y_att = _dot(o_ref[r, :], woa_ref[...])
        yg = jnp.concatenate([yg_ref[gb, r, :] for gb in range(S5_NB)], axis=-1).astype(BF16)
        y_ssm = _dot(yg, wglu_ref[:, :D_MODEL]) * jax.nn.sigmoid(_dot(yg, wglu_ref[:, D_MODEL:]))
        return ga_ref[r, :].astype(F32) * y_att + gs_ref[r, :].astype(F32) * y_ssm

    def project(r, merged):
        m = _dot(merged.astype(BF16), wout_ref[...])
        xo_ref[r, :] = x_ref[r, :] + _rms(m, gpost_ref[...])

    r0, r1 = slice(0, half), slice(half, 2 * half)
    merged0 = branches(r0)
    merged1 = branches(r1)
    project(r0, merged0)
    project(r1, merged1)


def _mix_out(x, layer, o, yg, ga, gs, woa, wglu, wout, gpost, tm=512):
    row = lambda d: pl.BlockSpec((None, tm, d), lambda b, i: (b, i, 0))
    full = lambda a: _layer_spec(a, layer)
    yg_spec = pl.BlockSpec((S5_NB, None, tm, LANES), lambda b, i: (0, b, i, 0))
    return pl.pallas_call(
        _mix_out_kernel,
        grid=(BATCH, SEQ // tm),
        in_specs=[row(D_MODEL), row(N_HEADS * V_HEAD), yg_spec, row(D_MODEL), row(D_MODEL),
                  full(woa), full(wglu), full(wout), full(gpost)],
        out_specs=row(D_MODEL),
        out_shape=jax.ShapeDtypeStruct((BATCH, SEQ, D_MODEL), F32),
        compiler_params=pltpu.CompilerParams(
            dimension_semantics=("arbitrary", "arbitrary"), vmem_limit_bytes=VMEM_LIMIT),
        name="mix_out",
    )(x, o, yg, ga, gs, woa, wglu, wout, gpost)


FFN_HALO = 8


def _ffn_kernel(x_ref, gpre_ref, wup_ref, cw_ref, cb_ref, wdn_ref, gpost_ref, xo_ref, gate_scr):
    tm = x_ref.shape[0]
    i = pl.program_id(1)

    @pl.when(i == 0)
    def _():
        gate_scr[0:FFN_HALO, :] = jnp.zeros((FFN_HALO, D_FF), F32)

    @pl.when(i > 0)
    def _():
        gate_scr[0:FFN_HALO, :] = gate_scr[tm:tm + FFN_HALO, :]

    x = x_ref[...]
    h = _rms(x, gpre_ref[...]).astype(BF16)
    gate = _dot(h, wup_ref[:, :D_FF])
    val = _dot(h, wup_ref[:, D_FF:])
    gate_scr[FFN_HALO:, :] = gate
    conv = (cb_ref[...]
            + cw_ref[0:1, :] * gate_scr[pl.ds(FFN_HALO - 2, tm), :]
            + cw_ref[1:2, :] * gate_scr[pl.ds(FFN_HALO - 1, tm), :]
            + cw_ref[2:3, :] * gate)
    act = (jax.nn.gelu(conv) * val).astype(BF16)
    y = _dot(act, wdn_ref[...])
    xo_ref[...] = x + _rms(y, gpost_ref[...])


def _ffn(x, layer, gpre, wup, cw, cb, wdn, gpost, tm=512):
    row = pl.BlockSpec((None, tm, D_MODEL), lambda b, i: (b, i, 0))
    full = lambda a: _layer_spec(a, layer)
    return pl.pallas_call(
        _ffn_kernel,
        grid=(BATCH, SEQ // tm),
        in_specs=[row, full(gpre), full(wup), full(cw), full(cb), full(wdn), full(gpost)],
        out_specs=row,
        out_shape=jax.ShapeDtypeStruct((BATCH, SEQ, D_MODEL), F32),
        scratch_shapes=[pltpu.VMEM((tm + FFN_HALO, D_FF), F32)],
        compiler_params=pltpu.CompilerParams(
            dimension_semantics=("arbitrary", "arbitrary"), vmem_limit_bytes=VMEM_LIMIT),
        name="ffn",
    )(x, gpre, wup, cw, cb, wdn, gpost)


W_IN_ROWS = 256


def _w_in_layout_kernel(w_ref, o_ref):
    o1 = Q_LORA
    o2 = o1 + KV_LORA
    o3 = o2 + QK_ROPE
    o4 = o3 + SSM_WIDTH
    o_ref[:, C_Q0:C_KPE0] = w_ref[:, :o1].astype(BF16)
    o_ref[:, C_KPE0:C_KV0] = jnp.zeros((W_IN_ROWS, C_KV0 - C_KPE0), BF16)
    o_ref[:, C_KPE0:C_KPE0 + QK_ROPE] = w_ref[:, o2:o3].astype(BF16)
    o_ref[:, C_KV0:C_U0] = w_ref[:, o1:o2].astype(BF16)
    o_ref[:, C_U0:C_GA0] = w_ref[:, o3:o4].astype(BF16)
    o_ref[:, C_GA0:C_END] = w_ref[:, o4:].astype(BF16)


def _w_in_layout(w_in):
    cols = w_in.shape[-1]
    return pl.pallas_call(
        _w_in_layout_kernel,
        grid=(DEPTH, D_MODEL // W_IN_ROWS),
        in_specs=[pl.BlockSpec((None, W_IN_ROWS, cols), lambda l, r: (l, r, 0))],
        out_specs=pl.BlockSpec((None, W_IN_ROWS, C_END), lambda l, r: (l, r, 0)),
        out_shape=jax.ShapeDtypeStruct((DEPTH, D_MODEL, C_END), BF16),
        compiler_params=pltpu.CompilerParams(
            dimension_semantics=("arbitrary", "arbitrary"), vmem_limit_bytes=VMEM_LIMIT),
        name="w_in_layout",
    )(w_in)


def _layout_w_uq(w):
    w4 = w.reshape(DEPTH, Q_LORA, N_HEADS, QK_NOPE + QK_ROPE)
    pad = jnp.zeros((DEPTH, Q_LORA, N_HEADS, HEAD_PAD - QK_NOPE - QK_ROPE), w.dtype)
    return jnp.concatenate([w4, pad], axis=-1).reshape(DEPTH, Q_LORA, N_HEADS * HEAD_PAD).astype(BF16)


def _layout_w_ukv(w):
    w4 = w.reshape(DEPTH, KV_LORA, N_HEADS, QK_NOPE + V_HEAD)
    return (w4[..., :QK_NOPE].reshape(DEPTH, KV_LORA, N_HEADS * QK_NOPE).astype(BF16),
            w4[..., QK_NOPE:].reshape(DEPTH, KV_LORA, N_HEADS * V_HEAD).astype(BF16))


def _rope_tables():
    pos = jnp.arange(SEQ, dtype=F32)
    inv_freq = ROPE_THETA ** (-jnp.arange(0, QK_ROPE, 2, dtype=F32) / QK_ROPE)
    ang = pos[:, None] * inv_freq[None, :]
    cos, sin = jnp.cos(ang), jnp.sin(ang)
    scale = math.log2(math.e) / math.sqrt(QK_NOPE + QK_ROPE)
    ones = jnp.ones((SEQ, QK_NOPE), F32)
    zq = jnp.zeros((SEQ, HEAD_PAD - QK_NOPE - QK_ROPE), F32)
    cq = jnp.concatenate([ones, cos, cos, zq], axis=1) * scale
    sq = jnp.concatenate([0.0 * ones, -sin, sin, zq], axis=1) * scale
    zk = jnp.zeros((SEQ, LANES - QK_ROPE), F32)
    ck = jnp.concatenate([cos, cos, zk], axis=1)
    sk = jnp.concatenate([-sin, sin, zk], axis=1)
    return cq, sq, ck, sk


def kernel(x, w_in, b_gate, g_mix_pre, g_q, w_uq, g_kv, w_ukv, w_o_att, lam_re, lam_im, log_step, b_re, b_im, c_re, c_im, d_skip, w_glu, w_out, g_mix_post, g_ffn_pre, w_up, conv_w, conv_b, w_down, g_ffn_post):
    cq, sq, ck, sk = _rope_tables()
    wsin, wout_s5, z_s5, a16 = _s5_prep(lam_re, lam_im, log_step, b_re, b_im, c_re, c_im)
    dsk = jnp.tile(d_skip.reshape(DEPTH * S5_NB, 1, LANES), (1, 1, S5_L))
    cw = jnp.pad(conv_w, ((0, 0), (0, FFN_HALO - CONV_W), (0, 0)))
    vec = lambda a: a[:, None, :]
    win = _w_in_layout(w_in)
    wuq = _layout_w_uq(w_uq)
    wkn, wv = _layout_w_ukv(w_ukv)
    woa, wglu, wout, wup, wdn = (a.astype(BF16) for a in (w_o_att, w_glu, w_out, w_up, w_down))
    gpre, bg, gq, gkv, gpost = vec(g_mix_pre), vec(b_gate), vec(g_q), vec(g_kv), vec(g_mix_post)
    fpre, cb, fpost = vec(g_ffn_pre), vec(conv_b), vec(g_ffn_post)
    for l in range(DEPTH):
        q, k, v, u, ga, gs = _mix_in(x, l, gpre, win, bg, gq, wuq, gkv, wkn, wv, cq, sq, ck, sk)
        o = _attn(q, k, v)
        yg = _s5(u, l, wsin, wout_s5, z_s5, a16, dsk)
        x = _mix_out(x, l, o, yg, ga, gs, woa, wglu, wout, gpost)
        x = _ffn(x, l, fpre, wup, cw, cb, wdn, fpost)
    return x
```
